```python
import jax, jax.numpy as jnp
from jax import lax
import numpy as np

D_MODEL = 1024
BATCH = 8
SEQ = 2048
DEPTH = 1

D_MIX = D_MODEL
ATT_HEADS = 4
ATT_HEAD_DIM = 128
ROPE_DIM = ATT_HEAD_DIM // 4
NOPE_DIM = ATT_HEAD_DIM - ROPE_DIM
KV_LATENT = 128
IDX_HEADS = 8
IDX_DIM = 64
IDX_ROPE_DIM = IDX_DIM // 4
IDX_TOPK_MAX = 256
Q_BLOCK = 128
LRU_WIDTH = D_MIX - ATT_HEADS * ATT_HEAD_DIM
LRU_BLOCKS = 8
LRU_BLOCK_DIM = LRU_WIDTH // LRU_BLOCKS
CONV_WIDTH = 4
LRU_C = 8.0
ROPE_THETA = 500000.0
PLE_DIM = 256
PEER_HEADS = 8
PEER_NKEYS = 128
PEER_EXPERTS = PEER_NKEYS * PEER_NKEYS
PEER_QDIM = 256
PEER_HALF = PEER_QDIM // 2
PEER_TOPK = 16
TOKEN_CHUNK = 128
EPS = 1e-6
IN_SIZES = (ATT_HEADS * ATT_HEAD_DIM, KV_LATENT, ROPE_DIM, IDX_HEADS * IDX_DIM, IDX_DIM, IDX_HEADS, LRU_WIDTH, LRU_WIDTH)
IN_TOTAL = sum(IN_SIZES)

kernel_name = 'hybrid_dsa_rglru_peer_block'


def rmsnorm(x, g):
    xf = x.astype(jnp.float32)
    y = xf * lax.rsqrt(jnp.mean(xf * xf, axis=-1, keepdims=True) + EPS)
    return (y * g.astype(jnp.float32)).astype(x.dtype)


def rope_angles(positions, dim):
    inv = ROPE_THETA ** (-jnp.arange(0, dim, 2, dtype=jnp.float32) / dim)
    ang = positions.astype(jnp.float32)[..., None] * inv
    return jnp.cos(ang), jnp.sin(ang)


def apply_rope(x, cos, sin):
    x1, x2 = jnp.split(x.astype(jnp.float32), 2, axis=-1)
    return jnp.concatenate([x1 * cos - x2 * sin, x2 * cos + x1 * sin], axis=-1).astype(x.dtype)


def partial_rope(x, cos, sin, rot):
    return jnp.concatenate([apply_rope(x[..., :rot], cos, sin), x[..., rot:]], axis=-1)


def dsa_attention(q, c_kv, k_rope, iq, ik, iw, g_kv, w_uk, w_uv, cos_a, sin_a, cos_i, sin_i):
    B, S = q.shape[0], q.shape[1]
    q = q.reshape(B, S, ATT_HEADS, ATT_HEAD_DIM)
    q_rope = apply_rope(q[..., :ROPE_DIM], cos_a[:, :, None], sin_a[:, :, None])
    q_lat = jnp.einsum('bshn,hcn->bshc', q[..., ROPE_DIM:], w_uk)
    q_cat = jnp.concatenate([q_lat, q_rope], axis=-1)
    kv_cat = jnp.concatenate([rmsnorm(c_kv, g_kv), apply_rope(k_rope, cos_a, sin_a)], axis=-1)
    iq = partial_rope(iq.reshape(B, S, IDX_HEADS, IDX_DIM), cos_i[:, :, None], sin_i[:, :, None], IDX_ROPE_DIM)
    ik = partial_rope(ik, cos_i, sin_i, IDX_ROPE_DIM)
    iw = iw * (IDX_HEADS ** -0.5 * IDX_DIM ** -0.5)
    top_k = min(IDX_TOPK_MAX, S // 4)
    n_blocks = S // Q_BLOCK
    scale = ATT_HEAD_DIM ** -0.5
    key_pos = jnp.arange(S)

    def to_blocks(a):
        return a.reshape(B, n_blocks, Q_BLOCK, *a.shape[2:]).swapaxes(0, 1)

    def block(args):
        qc, iqb, iwb, start = args
        t_pos = start + jnp.arange(Q_BLOCK)
        rel = jax.nn.relu(jnp.einsum('bthd,bsd->bths', iqb, ik))
        score = jnp.einsum('bths,bth->bts', rel, iwb).astype(jnp.float32)
        causal = key_pos[None, :] <= t_pos[:, None]
        score = jnp.where(causal[None], score, -jnp.inf)
        _, sel = lax.top_k(score, top_k)
        kv_sel = jax.vmap(lambda kv, i: kv[i])(kv_cat, sel)
        logits = jnp.einsum('bthc,btkc->bthk', qc, kv_sel).astype(jnp.float32) * scale
        valid = sel <= t_pos[None, :, None]
        logits = jnp.where(valid[:, :, None, :], logits, -jnp.inf)
        probs = jax.nn.softmax(logits, axis=-1).astype(qc.dtype)
        o_lat = jnp.einsum('bthk,btkc->bthc', probs, kv_sel[..., :KV_LATENT])
        return jnp.einsum('bthc,hcd->bthd', o_lat, w_uv).reshape(B, Q_BLOCK, ATT_HEADS * ATT_HEAD_DIM)

    starts = jnp.arange(n_blocks, dtype=jnp.int32) * Q_BLOCK
    out = lax.map(block, (to_blocks(q_cat), to_blocks(iq), to_blocks(iw), starts))
    return out.swapaxes(0, 1).reshape(B, S, ATT_HEADS * ATT_HEAD_DIM)


def rg_lru_branch(xb, gate, conv_w, conv_b, w_rg, b_rg, w_ig, b_ig, lam):
    B, S, C = xb.shape
    xc = lax.conv_general_dilated(xb, conv_w[:, None, :], window_strides=(1,), padding=((CONV_WIDTH - 1, 0),),
                                  dimension_numbers=('NWC', 'WIO', 'NWC'), feature_group_count=C) + conv_b
    xblk = xc.reshape(B, S, LRU_BLOCKS, LRU_BLOCK_DIM)
    r = jax.nn.sigmoid(jnp.einsum('bsni,nio->bsno', xblk, w_rg).reshape(B, S, C) + b_rg)
    i = jax.nn.sigmoid(jnp.einsum('bsni,nio->bsno', xblk, w_ig).reshape(B, S, C) + b_ig)
    log_a = -LRU_C * r.astype(jnp.float32) * jax.nn.softplus(-lam.astype(jnp.float32))
    a = jnp.exp(log_a)
    b = jnp.sqrt(1.0 - jnp.exp(2.0 * log_a)) * (i * xc).astype(jnp.float32)

    def combine(left, right):
        a1, b1 = left
        a2, b2 = right
        return a1 * a2, a2 * b1 + b2

    _, h = lax.associative_scan(combine, (a, b), axis=1)
    return h.astype(xb.dtype) * jax.nn.gelu(gate)


def peer_ffn(h, w_pq, k1, k2, u_tab, v_tab):
    B, S, D = h.shape
    q = (h @ w_pq).reshape(B, S, PEER_HEADS, 2, PEER_HALF)
    s1 = jnp.einsum('bshd,hnd->bshn', q[..., 0, :], k1)
    s2 = jnp.einsum('bshd,hnd->bshn', q[..., 1, :], k2)
    v1, i1 = lax.top_k(s1, PEER_TOPK)
    v2, i2 = lax.top_k(s2, PEER_TOPK)
    cand = (v1[..., :, None] + v2[..., None, :]).reshape(B, S, PEER_HEADS, PEER_TOPK * PEER_TOPK)
    vals, flat = lax.top_k(cand, PEER_TOPK)
    e1 = jnp.take_along_axis(i1, flat // PEER_TOPK, axis=-1)
    e2 = jnp.take_along_axis(i2, flat % PEER_TOPK, axis=-1)
    experts = e1 * PEER_NKEYS + e2
    gates = jax.nn.softmax(vals.astype(jnp.float32), axis=-1).astype(h.dtype)
    n_chunks = (B * S) // TOKEN_CHUNK
    hc = h.reshape(n_chunks, TOKEN_CHUNK, D)
    ec = experts.reshape(n_chunks, TOKEN_CHUNK, PEER_HEADS * PEER_TOPK)
    gc = gates.reshape(n_chunks, TOKEN_CHUNK, PEER_HEADS * PEER_TOPK)

    def chunk(args):
        hx, e, g = args
        act = jax.nn.gelu(jnp.einsum('ckd,cd->ck', u_tab[e], hx), approximate=False)
        return jnp.einsum('ck,ckd->cd', g * act, v_tab[e])

    return lax.map(chunk, (hc, ec, gc)).reshape(B, S, D)


def setup_inputs(seed: int = 0) -> dict:
    key = jax.random.key(seed)
    ks = jax.random.split(key, 32)
    f32 = jnp.float32
    nrm = lambda k, shape, s: jax.random.normal(k, shape, f32) * s
    u = jax.random.uniform(ks[14], (DEPTH, LRU_WIDTH), f32, 0.9, 0.999)
    sg = u ** (1.0 / LRU_C)
    lam = jnp.log(sg) - jnp.log1p(-sg)
    return {
        'x': nrm(ks[0], (BATCH, SEQ, D_MODEL), 1.0),
        'p': nrm(ks[1], (DEPTH, BATCH, SEQ, PLE_DIM), 1.0),
        'positions': jnp.broadcast_to(jnp.arange(SEQ, dtype=jnp.int32), (BATCH, SEQ)),
        'g_mix': 1.0 + nrm(ks[2], (DEPTH, D_MODEL), 0.01),
        'w_in': nrm(ks[3], (DEPTH, D_MODEL, IN_TOTAL), D_MODEL ** -0.5),
        'g_kv': 1.0 + nrm(ks[4], (DEPTH, KV_LATENT), 0.01),
        'w_uk': nrm(ks[5], (DEPTH, ATT_HEADS, KV_LATENT, NOPE_DIM), KV_LATENT ** -0.5),
        'w_uv': nrm(ks[6], (DEPTH, ATT_HEADS, KV_LATENT, ATT_HEAD_DIM), KV_LATENT ** -0.5),
        'conv_w': nrm(ks[7], (DEPTH, CONV_WIDTH, LRU_WIDTH), CONV_WIDTH ** -0.5),
        'conv_b': nrm(ks[8], (DEPTH, LRU_WIDTH), 0.01),
        'w_rg': nrm(ks[9], (DEPTH, LRU_BLOCKS, LRU_BLOCK_DIM, LRU_BLOCK_DIM), LRU_BLOCK_DIM ** -0.5),
        'b_rg': nrm(ks[10], (DEPTH, LRU_WIDTH), 0.01),
        'w_ig': nrm(ks[11], (DEPTH, LRU_BLOCKS, LRU_BLOCK_DIM, LRU_BLOCK_DIM), LRU_BLOCK_DIM ** -0.5),
        'b_ig': nrm(ks[12], (DEPTH, LRU_WIDTH), 0.01),
        'lru_lambda': lam,
        'w_out': nrm(ks[13], (DEPTH, D_MIX, D_MODEL), D_MIX ** -0.5),
        'g_ffn': 1.0 + nrm(ks[15], (DEPTH, D_MODEL), 0.01),
        'w_pq': nrm(ks[16], (DEPTH, D_MODEL, PEER_HEADS * PEER_QDIM), D_MODEL ** -0.5),
        'peer_k1': nrm(ks[17], (DEPTH, PEER_HEADS, PEER_NKEYS, PEER_HALF), PEER_HALF ** -0.5),
        'peer_k2': nrm(ks[18], (DEPTH, PEER_HEADS, PEER_NKEYS, PEER_HALF), PEER_HALF ** -0.5),
        'peer_u': nrm(ks[19], (DEPTH, PEER_EXPERTS, D_MODEL), D_MODEL ** -0.5),
        'peer_v': nrm(ks[20], (DEPTH, PEER_EXPERTS, D_MODEL), (PEER_HEADS * PEER_TOPK) ** -0.5),
        'w_ple': nrm(ks[21], (DEPTH, PLE_DIM, D_MODEL), PLE_DIM ** -0.5),
        'w_ple_gate': nrm(ks[22], (DEPTH, D_MODEL, D_MODEL), D_MODEL ** -0.5),
        'g_final': 1.0 + nrm(ks[23], (D_MODEL,), 0.01),
    }


def reference(x, p, positions, g_mix, w_in, g_kv, w_uk, w_uv, conv_w, conv_b, w_rg, b_rg, w_ig, b_ig,
              lru_lambda, w_out, g_ffn, w_pq, peer_k1, peer_k2, peer_u, peer_v, w_ple, w_ple_gate, g_final):
    cos_a, sin_a = rope_angles(positions, ROPE_DIM)
    cos_i, sin_i = rope_angles(positions, IDX_ROPE_DIM)
    splits = [int(s) for s in np.cumsum(IN_SIZES)[:-1]]
    for i in range(DEPTH):
        h = rmsnorm(x, g_mix[i])
        q, c_kv, k_rope, iq, ik, iw, xb, gate = jnp.split(h @ w_in[i], splits, axis=-1)
        att = dsa_attention(q, c_kv, k_rope, iq, ik, iw, g_kv[i], w_uk[i], w_uv[i], cos_a, sin_a, cos_i, sin_i)
        rec = rg_lru_branch(xb, gate, conv_w[i], conv_b[i], w_rg[i], b_rg[i], w_ig[i], b_ig[i], lru_lambda[i])
        x = x + jnp.concatenate([att, rec], axis=-1) @ w_out[i]
        x = x + peer_ffn(rmsnorm(x, g_ffn[i]), w_pq[i], peer_k1[i], peer_k2[i], peer_u[i], peer_v[i])
        x = x + jax.nn.sigmoid(x @ w_ple_gate[i]) * (p[i] @ w_ple[i])
    return rmsnorm(x, g_final)
```

```python
import functools

import jax
import jax.numpy as jnp
from jax import lax
from jax.experimental import pallas as pl
from jax.experimental.pallas import tpu as pltpu

D_MODEL = 1024
ATT_HEADS = 4
ATT_HEAD_DIM = 128
ROPE_DIM = 32
NOPE_DIM = ATT_HEAD_DIM - ROPE_DIM
KV_LATENT = 128
IDX_HEADS = 8
IDX_DIM = 64
IDX_ROPE_DIM = 16
IDX_TOPK_MAX = 256
LRU_WIDTH = 512
LRU_BLOCKS = 8
LRU_BLOCK_DIM = LRU_WIDTH // LRU_BLOCKS
CONV_WIDTH = 4
LRU_C = 8.0
ROPE_THETA = 500000.0
PLE_DIM = 256
PEER_HEADS = 8
PEER_NKEYS = 128
PEER_EXPERTS = PEER_NKEYS * PEER_NKEYS
PEER_HALF = 128
PEER_TOPK = 16
EPS = 1e-6

LANES = 128
SUBLANES = 8
BF16_ROWS = 16
INT_MIN = -2147483648
VMEM_LIMIT = 56 * 1024 * 1024

F32 = jnp.float32
BF16 = jnp.bfloat16

_C_QN = 0
_C_QR = _C_QN + 384
_C_QRR = _C_QR + 128
_C_CKV = _C_QRR + 128
_C_KR = _C_CKV + 128
_C_KRR = _C_KR + 128
_C_IQ = _C_KRR + 128
_C_IQR = _C_IQ + 512
_C_IK = _C_IQR + 512
_C_IKR = _C_IK + 128
_C_IW = _C_IKR + 128
_C_XB = _C_IW + 128
_C_GATE = _C_XB + 512
_C_TOTAL = _C_GATE + 512


def _cparams(sem):
    return pltpu.CompilerParams(dimension_semantics=sem, vmem_limit_bytes=VMEM_LIMIT)


def _rms(x, g):
    return x * lax.rsqrt(jnp.mean(x * x, axis=-1, keepdims=True) + EPS) * g


def _dot(a, b):
    return jnp.dot(a, b, preferred_element_type=F32)


def _dot_nt(a, b):
    return lax.dot_general(a, b, (((1,), (1,)), ((), ())), preferred_element_type=F32)


def _rot_half(w, half):
    return jnp.concatenate([-w[:, half:2 * half], w[:, :half]], axis=1)


def _pack_in_weights(w_in, w_uk):
    d = w_in.shape[0]
    o = 0
    wq = w_in[:, o:o + 512]; o += 512
    wckv = w_in[:, o:o + 128]; o += 128
    wkr = w_in[:, o:o + 32]; o += 32
    wiq = w_in[:, o:o + 512]; o += 512
    wik = w_in[:, o:o + 64]; o += 64
    wiw = w_in[:, o:o + 8]; o += 8
    wxb = w_in[:, o:o + 512]; o += 512
    wgate = w_in[:, o:o + 512]

    qn, qr, qrr = [], [], []
    for h in range(ATT_HEADS):
        wh = wq[:, h * 128:(h + 1) * 128]
        qr.append(wh[:, :ROPE_DIM])
        qrr.append(_rot_half(wh[:, :ROPE_DIM], ROPE_DIM // 2))
        qn.append(wh[:, ROPE_DIM:])
    iqr = []
    zpad = jnp.zeros((d, IDX_DIM - IDX_ROPE_DIM), w_in.dtype)
    for h in range(IDX_HEADS):
        wh = wiq[:, h * 64:(h + 1) * 64]
        iqr.append(jnp.concatenate([_rot_half(wh[:, :IDX_ROPE_DIM], IDX_ROPE_DIM // 2), zpad], axis=1))
    ikr = jnp.concatenate([_rot_half(wik[:, :IDX_ROPE_DIM], IDX_ROPE_DIM // 2), zpad], axis=1)
    cols = [
        jnp.concatenate(qn, axis=1),
        jnp.concatenate(qr, axis=1),
        jnp.concatenate(qrr, axis=1),
        wckv,
        jnp.tile(wkr, (1, 4)),
        jnp.tile(_rot_half(wkr, ROPE_DIM // 2), (1, 4)),
        wiq,
        jnp.concatenate(iqr, axis=1),
        jnp.tile(wik, (1, 2)),
        jnp.tile(ikr, (1, 2)),
        jnp.concatenate([wiw, jnp.zeros((d, 120), w_in.dtype)], axis=1),
        wxb,
        wgate,
    ]
    w_all = jnp.concatenate(cols, axis=1).astype(BF16)
    assert w_all.shape[1] == _C_TOTAL
    wuk_bd = jnp.zeros((ATT_HEADS * NOPE_DIM, ATT_HEADS * KV_LATENT), F32)
    for h in range(ATT_HEADS):
        wuk_bd = wuk_bd.at[h * NOPE_DIM:(h + 1) * NOPE_DIM, h * KV_LATENT:(h + 1) * KV_LATENT].set(w_uk[h].T)
    return w_all, wuk_bd.astype(BF16)


def _rope_freq_rows():
    inv_a = ROPE_THETA ** (-jnp.arange(0, ROPE_DIM, 2, dtype=F32) / ROPE_DIM)
    inv_i = ROPE_THETA ** (-jnp.arange(0, IDX_ROPE_DIM, 2, dtype=F32) / IDX_ROPE_DIM)
    row_a = jnp.tile(inv_a, LANES // inv_a.shape[0])[None, :]
    head_i = jnp.concatenate([inv_i, inv_i, jnp.zeros((IDX_DIM - IDX_ROPE_DIM,), F32)])
    row_i = jnp.tile(head_i, LANES // IDX_DIM)[None, :]
    return row_a, row_i


def _inproj_kernel(x_ref, pos_ref, g_ref, fa_ref, fi_ref, w_ref, wuk_ref, gkv_ref,
                   qcat_ref, kvcat_ref, iq_ref, ik_ref, iw_ref, xb_ref, gate_ref):
    hb = _rms(x_ref[...], g_ref[...]).astype(BF16)
    pos = pos_ref[...]

    def proj(c0, width):
        return _dot(hb, w_ref[:, c0:c0 + width])

    ang_a = pos * fa_ref[...]
    cos_a, sin_a = jnp.cos(ang_a), jnp.sin(ang_a)
    ang_i = pos * fi_ref[...]
    cos_i, sin_i = jnp.cos(ang_i), jnp.sin(ang_i)

    q_lat = _dot(proj(_C_QN, 384).astype(BF16), wuk_ref[...])
    q_rope = proj(_C_QR, 128) * cos_a + proj(_C_QRR, 128) * sin_a
    lane = lax.broadcasted_iota(jnp.int32, q_rope.shape, 1)
    for h in range(ATT_HEADS):
        qcat_ref[h, :, 0:128] = q_lat[:, h * 128:(h + 1) * 128].astype(BF16)
        keep = (lane >> 5) == h
        qcat_ref[h, :, 128:256] = jnp.where(keep, q_rope, 0.0).astype(BF16)

    kvcat_ref[:, 0:128] = _rms(proj(_C_CKV, 128), gkv_ref[...]).astype(BF16)
    kvcat_ref[:, 128:256] = (proj(_C_KR, 128) * cos_a + proj(_C_KRR, 128) * sin_a).astype(BF16)

    for c in range(4):
        sl = slice(c * 128, (c + 1) * 128)
        iq_ref[:, sl] = (proj(_C_IQ + c * 128, 128) * cos_i + proj(_C_IQR + c * 128, 128) * sin_i).astype(BF16)
    ik_ref[...] = (proj(_C_IK, 128) * cos_i + proj(_C_IKR, 128) * sin_i).astype(BF16)
    iw_ref[...] = proj(_C_IW, 128) * (IDX_HEADS ** -0.5 * IDX_DIM ** -0.5)
    xb_ref[...] = proj(_C_XB, 512)
    gate_ref[...] = proj(_C_GATE, 512)


def _in_projection(x2, posf, g_mix, w_all, wuk_bd, g_kv, tm):
    n = x2.shape[0]
    row_a, row_i = _rope_freq_rows()
    tok = lambda w: pl.BlockSpec((tm, w), lambda i: (i, 0))
    full = lambda a: pl.BlockSpec(a.shape, lambda i: (0,) * a.ndim)
    outs = [
        jax.ShapeDtypeStruct((ATT_HEADS, n, 256), BF16),
        jax.ShapeDtypeStruct((n, 256), BF16),
        jax.ShapeDtypeStruct((n, 512), BF16),
        jax.ShapeDtypeStruct((n, 128), BF16),
        jax.ShapeDtypeStruct((n, 128), F32),
        jax.ShapeDtypeStruct((n, 512), F32),
        jax.ShapeDtypeStruct((n, 512), F32),
    ]
    g2 = g_mix[None, :]
    gkv2 = g_kv[None, :]
    return pl.pallas_call(
        _inproj_kernel,
        grid=(n // tm,),
        in_specs=[tok(D_MODEL), tok(1), full(g2), full(row_a), full(row_i), full(w_all), full(wuk_bd), full(gkv2)],
        out_specs=[pl.BlockSpec((ATT_HEADS, tm, 256), lambda i: (0, i, 0))] + [tok(s.shape[1]) for s in outs[1:]],
        out_shape=outs,
        compiler_params=_cparams(("parallel",)),
        name="in_projection",
    )(x2, posf, g2, row_a, row_i, w_all, wuk_bd, gkv2)


def _ordered_bits_to_float(u):
    s = u ^ INT_MIN
    return pltpu.bitcast(s ^ ((s >> 31) & jnp.int32(0x7FFFFFFF)), F32)


def _tree(op, xs):
    while len(xs) > 1:
        xs = [op(xs[i], xs[i + 1]) if i + 1 < len(xs) else xs[i] for i in range(0, len(xs), 2)]
    return xs[0]


def _col_sum8(x):
    return jnp.sum(x.reshape(x.shape[0] // SUBLANES, SUBLANES, x.shape[1]), axis=0)


def _dsa_kernel(iq_ref, iw_ref, qcat_ref, ik_ref, kvcat_ref, kvt_ref, wuvt_ref, out_ref,
                iqm_s, score_s, lg_s, o_s, fv_s, fp_s, *, seq, tq, kc, topk):
    j = pl.program_id(1)
    t0 = j * tq
    nk = (t0 + tq + kc - 1) // kc
    tpos = t0 + lax.broadcasted_iota(jnp.int32, (kc, tq), 1)
    row = lax.broadcasted_iota(jnp.int32, (kc, tq), 0)

    lane = lax.broadcasted_iota(jnp.int32, (tq, LANES), 1)
    for h in range(IDX_HEADS):
        pair = iq_ref[0, :, (h // 2) * 128:(h // 2 + 1) * 128]
        keep = (lane < IDX_DIM) if h % 2 == 0 else (lane >= IDX_DIM)
        iqm_s[h * tq:(h + 1) * tq, :] = jnp.where(keep, pair, jnp.zeros_like(pair))
    iw_row = iw_ref[0, 0]

    def score_chunk(c, carry):
        k0 = pl.multiple_of(c * kc, kc)
        r = jnp.maximum(_dot_nt(ik_ref[0, pl.ds(k0, kc), :], iqm_s[...]), 0.0) * iw_row
        sc = r[:, 0:tq]
        for h in range(1, IDX_HEADS):
            sc = sc + r[:, h * tq:(h + 1) * tq]
        score_s[pl.ds(k0, kc), :] = jnp.where(k0 + row <= tpos, sc, -jnp.inf)
        return carry

    lax.fori_loop(0, nk, score_chunk, 0)

    def fold(fn, op, reduce_rows, init):
        def body(c, acc):
            k0 = pl.multiple_of(c * kc, kc)
            v = fn(score_s[pl.ds(k0, kc), :], k0)
            return op(acc, _tree(op, [v[g * SUBLANES:(g + 1) * SUBLANES] for g in range(kc // SUBLANES)]))
        acc = lax.fori_loop(0, nk, body, jnp.full((SUBLANES, tq), init, jnp.asarray(init).dtype))
        return reduce_rows(acc, axis=0, keepdims=True)

    def count(pred):
        return fold(lambda blk, k0: pred(blk, k0).astype(jnp.int32), jnp.add, jnp.sum, jnp.int32(0))

    def bit_pass(i, t_u):
        cand_u = t_u | jnp.left_shift(jnp.int32(1), 31 - i)
        cand = _ordered_bits_to_float(cand_u)
        return jnp.where(count(lambda blk, k0: blk >= cand) >= topk, cand_u, t_u)

    t_u = lax.fori_loop(0, 32, bit_pass, jnp.zeros((1, tq), jnp.int32))
    t_u = jnp.where((t_u >> 23) == 0, jnp.int32(0x00800000), t_u)
    thr = _ordered_bits_to_float(t_u)

    n_ge = count(lambda blk, k0: blk >= thr)
    fv_s[...] = thr
    fp_s[...] = jnp.full((1, tq), seq, jnp.int32)
    extra = jnp.max(n_ge) - topk

    @pl.when(extra > 0)
    def _():
        def drop_one(i, carry):
            fv, fp, kept = carry

            def kept_scores(blk, k0):
                pos = k0 + row
                return jnp.where(blk == fv, jnp.where(pos < fp, blk, jnp.inf), jnp.where(blk > fv, blk, jnp.inf))

            low = fold(kept_scores, jnp.minimum, jnp.min, jnp.float32(jnp.inf))

            def kept_pos_at_low(blk, k0):
                pos = k0 + row
                kept_pos = jnp.where(blk == fv, jnp.where(pos < fp, pos, -1), jnp.where(blk > fv, pos, -1))
                return jnp.where(blk == low, kept_pos, -1)

            last = fold(kept_pos_at_low, jnp.maximum, jnp.max, jnp.int32(-1))
            over = kept > topk
            return jnp.where(over, low, fv), jnp.where(over, last, fp), jnp.where(over, kept - 1, kept)

        fv, fp, _ = lax.fori_loop(0, extra, drop_one, (thr, jnp.full((1, tq), seq, jnp.int32), n_ge))
        fv_s[...] = fv
        fp_s[...] = fp

    floor_v = fv_s[...]
    floor_p = fp_s[...]
    qcat = qcat_ref[:, 0].reshape(ATT_HEADS * tq, 256)
    scale = ATT_HEAD_DIM ** -0.5

    def logit_chunk(c, m8):
        k0 = pl.multiple_of(c * kc, kc)
        blk = score_s[pl.ds(k0, kc), :]
        sel = jnp.where(blk == floor_v, (k0 + row < floor_p).astype(jnp.int32), (blk > floor_v).astype(jnp.int32))
        bias = jnp.where(sel > 0, 0.0, -jnp.inf)
        lg = _dot_nt(kvcat_ref[0, pl.ds(k0, kc), :], qcat) * scale
        lg = lg + jnp.concatenate([bias] * ATT_HEADS, axis=1)
        lg_s[pl.ds(k0, kc), :] = lg
        return jnp.maximum(m8, jnp.max(lg.reshape(kc // SUBLANES, SUBLANES, ATT_HEADS * tq), axis=0))

    m8 = lax.fori_loop(0, nk, logit_chunk, jnp.full((SUBLANES, ATT_HEADS * tq), -jnp.inf, F32))
    m = jnp.max(m8, axis=0, keepdims=True)
    o_s[...] = jnp.zeros_like(o_s)

    def pv_chunk(c, den8):
        k0 = pl.multiple_of(c * kc, kc)
        p = jnp.exp(lg_s[pl.ds(k0, kc), :] - m)
        o_s[...] += _dot(kvt_ref[0, c], p.astype(BF16))
        return den8 + _col_sum8(p)

    den8 = lax.fori_loop(0, nk, pv_chunk, jnp.zeros((SUBLANES, ATT_HEADS * tq), F32))
    o = (o_s[...] / jnp.sum(den8, axis=0, keepdims=True)).astype(BF16)
    outs = [_dot(wuvt_ref[h], o[:, h * tq:(h + 1) * tq]) for h in range(ATT_HEADS)]
    out_ref[0] = jnp.concatenate(outs, axis=0).T.astype(BF16)


def _dsa_attention(iq, iw_rows, qcat, ik2, kvcat, kvt, wuvt, tq, kc):
    b, s, _ = iq.shape
    topk = min(IDX_TOPK_MAX, s // 4)
    assert s & (s - 1) == 0, "sequence length must be a power of two"
    assert kc >= topk and s % kc == 0 and kc % tq == 0
    qb = lambda w: pl.BlockSpec((1, tq, w), lambda i, j: (i, j, 0))
    kb = lambda w: pl.BlockSpec((1, s, w), lambda i, j: (i, 0, 0))
    return pl.pallas_call(
        functools.partial(_dsa_kernel, seq=s, tq=tq, kc=kc, topk=topk),
        grid=(b, s // tq),
        in_specs=[qb(512),
                  pl.BlockSpec((1, 1, 1, IDX_HEADS * tq), lambda i, j: (i, j, 0, 0)),
                  pl.BlockSpec((ATT_HEADS, 1, tq, 256), lambda i, j: (0, i, j, 0)),
                  kb(128), kb(256),
                  pl.BlockSpec((1, s // kc, KV_LATENT, kc), lambda i, j: (i, 0, 0, 0)),
                  pl.BlockSpec(wuvt.shape, lambda i, j: (0, 0, 0))],
        out_specs=qb(512),
        out_shape=jax.ShapeDtypeStruct((b, s, 512), BF16),
        scratch_shapes=[pltpu.VMEM((IDX_HEADS * tq, LANES), BF16),
                        pltpu.VMEM((s, tq), F32),
                        pltpu.VMEM((s, ATT_HEADS * tq), F32),
                        pltpu.VMEM((KV_LATENT, ATT_HEADS * tq), F32),
                        pltpu.VMEM((1, tq), F32), pltpu.VMEM((1, tq), jnp.int32)],
        compiler_params=_cparams(("parallel", "parallel")),
        name="dsa_attention",
    )(iq, iw_rows, qcat, ik2, kvcat, kvt, wuvt)


def _lru_kernel(xb_ref, gate_ref, cw_ref, cb_ref, wrg_ref, brg_ref, wig_ref, big_ref, lam_ref, out_ref,
                xpad_s, h_s, *, ts):
    @pl.when(pl.program_id(1) == 0)
    def _():
        xpad_s[0:8, :] = jnp.zeros((8, LRU_WIDTH), F32)
        h_s[...] = jnp.zeros_like(h_s)

    xb = xb_ref[0]
    xpad_s[8:8 + ts, :] = xb
    xc = cb_ref[...]
    for w in range(CONV_WIDTH):
        off = 8 - (CONV_WIDTH - 1) + w
        xc = xc + cw_ref[w:w + 1, :] * xpad_s[off:off + ts, :]
    xpad_s[0:8, :] = xb[ts - 8:ts, :]

    xcb = xc.astype(BF16)
    r = jax.nn.sigmoid(_dot(xcb, wrg_ref[...]) + brg_ref[...])
    i = jax.nn.sigmoid(_dot(xcb, wig_ref[...]) + big_ref[...])
    log_a = -LRU_C * r * jax.nn.softplus(-lam_ref[...])
    a = jnp.exp(log_a)
    bv = jnp.sqrt(1.0 - jnp.exp(2.0 * log_a)) * (i * xc)

    row = lax.broadcasted_iota(jnp.int32, (ts, LRU_WIDTH), 0)
    d = 1
    while d < ts:
        a_sh = jnp.where(row >= d, pltpu.roll(a, d, 0), 1.0)
        b_sh = jnp.where(row >= d, pltpu.roll(bv, d, 0), 0.0)
        bv = a * b_sh + bv
        a = a * a_sh
        d *= 2
    h = a * h_s[...] + bv
    h_s[...] = h[ts - 1:ts, :]
    out_ref[0] = (h * jax.nn.gelu(gate_ref[0])).astype(BF16)


def _block_diag(w):
    nb, bi, bo = w.shape
    out = jnp.zeros((nb * bi, nb * bo), w.dtype)
    for k in range(nb):
        out = out.at[k * bi:(k + 1) * bi, k * bo:(k + 1) * bo].set(w[k])
    return out


def _rg_lru(xb, gate, conv_w, conv_b, w_rg, b_rg, w_ig, b_ig, lam, ts):
    b, s, c = xb.shape
    row = lambda a: a[None, :]
    args = [conv_w, row(conv_b), _block_diag(w_rg).astype(BF16), row(b_rg),
            _block_diag(w_ig).astype(BF16), row(b_ig), row(lam)]
    tb = pl.BlockSpec((1, ts, c), lambda i, j: (i, j, 0))
    full = lambda a: pl.BlockSpec(a.shape, lambda i, j: (0,) * a.ndim)
    return pl.pallas_call(
        functools.partial(_lru_kernel, ts=ts),
        grid=(b, s // ts),
        in_specs=[tb, tb] + [full(a) for a in args],
        out_specs=tb,
        out_shape=jax.ShapeDtypeStruct((b, s, c), BF16),
        scratch_shapes=[pltpu.VMEM((ts + 8, c), F32), pltpu.VMEM((1, c), F32)],
        compiler_params=_cparams(("parallel", "arbitrary")),
        name="rg_lru",
    )(xb, gate, *args)


def _outproj_kernel(x_ref, att_ref, rec_ref, wo_ref, g_ref, wpq_ref, k1_ref, k2_ref,
                    x1_ref, h2t_ref, s1_ref, s2_ref):
    x1 = x_ref[...] + _dot(att_ref[...], wo_ref[0:512, :]) + _dot(rec_ref[...], wo_ref[512:1024, :])
    x1_ref[...] = x1
    h2f = _rms(x1, g_ref[...])
    h2t_ref[...] = h2f.T.astype(BF16)
    h2 = h2f.astype(BF16)
    for h in range(PEER_HEADS):
        q = _dot(h2, wpq_ref[:, h * 256:(h + 1) * 256]).astype(BF16)
        s1_ref[h] = _dot_nt(k1_ref[h], q[:, :PEER_HALF])
        s2_ref[h] = _dot_nt(k2_ref[h], q[:, PEER_HALF:])


def _out_projection(x2, att, rec, w_out, g_ffn, w_pq, k1, k2, tm):
    n = x2.shape[0]
    g2 = g_ffn[None, :]
    tok = lambda w: pl.BlockSpec((tm, w), lambda i: (i, 0))
    full = lambda a: pl.BlockSpec(a.shape, lambda i: (0,) * a.ndim)
    sblk = pl.BlockSpec((PEER_HEADS, PEER_NKEYS, tm), lambda i: (0, 0, i))
    s_shape = jax.ShapeDtypeStruct((PEER_HEADS, PEER_NKEYS, n), F32)
    return pl.pallas_call(
        _outproj_kernel,
        grid=(n // tm,),
        in_specs=[tok(D_MODEL), tok(512), tok(512), full(w_out), full(g2), full(w_pq), full(k1), full(k2)],
        out_specs=[tok(D_MODEL), pl.BlockSpec((D_MODEL, tm), lambda i: (0, i)), sblk, sblk],
        out_shape=[jax.ShapeDtypeStruct((n, D_MODEL), F32), jax.ShapeDtypeStruct((D_MODEL, n), BF16),
                   s_shape, s_shape],
        compiler_params=_cparams(("parallel",)),
        name="out_projection",
    )(x2, att, rec, w_out, g2, w_pq, k1, k2)


def _top_values(s, k):
    vals = []
    for _ in range(k):
        m = jnp.max(s, axis=0, keepdims=True)
        vals.append(m)
        s = jnp.where(s == m, -jnp.inf, s)
    return vals


def _kth_largest(s, k):
    kth = None
    above = jnp.zeros((1, s.shape[1]), F32)
    for _ in range(k):
        m = jnp.max(s, axis=0, keepdims=True)
        hit = s == m
        kth = m if kth is None else jnp.where(above < k, m, kth)
        above = above + jnp.sum(jnp.where(hit, 1.0, 0.0), axis=0, keepdims=True)
        s = jnp.where(hit, -jnp.inf, s)
    return kth


def _peer_select_kernel(s1_ref, s2_ref, cnt_ref, p1_ref, rank_ref, p2_ref, v1_s, v2_s):
    s1 = s1_ref[0]
    s2 = s2_ref[0]
    t = s1.shape[1]
    v1 = _top_values(s1, PEER_TOPK)
    for i in range(PEER_TOPK):
        v1_s[i:i + 1, :] = v1[i]
    v1_all = v1_s[...]
    v2 = []
    cur = s2
    rank = jnp.full(s2.shape, float(PEER_TOPK), F32)
    for j in range(PEER_TOPK):
        m = jnp.max(cur, axis=0, keepdims=True)
        v2.append(m)
        hit = cur == m
        rank = jnp.where(hit, float(j), rank)
        cur = jnp.where(hit, -jnp.inf, cur)
        v2_s[j:j + 1, :] = m
    v2_all = v2_s[...]

    m1, m2 = v1[0], v2[0]
    e1_all = jnp.exp(v1_all - m1)
    e2_all = jnp.exp(v2_all - m2)
    row8 = lax.broadcasted_iota(jnp.int32, (SUBLANES, t), 0)
    slabs = [(v1[0] + v2_all[0:8], e1_all[0:1] * e2_all[0:8]),
             (v1[0] + v2_all[8:16], e1_all[0:1] * e2_all[8:16])]
    for i in range(1, 8):
        keep = row8 < PEER_TOPK // (i + 1)
        slabs.append((jnp.where(keep, v1[i] + v2_all[0:8], -jnp.inf), e1_all[i:i + 1] * e2_all[0:8]))
    slabs.append((v1_all[8:16] + v2[0], e1_all[8:16] * e2_all[0:1]))
    cand = jnp.concatenate([c for c, _ in slabs], axis=0)
    thr = _kth_largest(cand, PEER_TOPK)
    z = jnp.sum(jnp.where(cand >= thr, jnp.concatenate([e for _, e in slabs], axis=0), 0.0), axis=0, keepdims=True)

    cnt = jnp.zeros(s1.shape, F32)
    for j in range(PEER_TOPK):
        cnt = cnt + jnp.where(s1 + v2[j] >= thr, 1.0, 0.0)
    cnt_ref[0] = cnt
    p1_ref[0] = jnp.exp(s1 - m1) / z
    p2 = jnp.exp(s2 - m2)
    for c in range(t // LANES):
        rank_ref[0, c] = rank[:, c * LANES:(c + 1) * LANES].astype(BF16)
        p2_ref[0, c] = p2[:, c * LANES:(c + 1) * LANES].astype(BF16)


def _peer_select(s1t, s2t, tb):
    n = s1t.shape[2]
    blk = pl.BlockSpec((1, PEER_NKEYS, tb), lambda h, i: (h, 0, i))
    f32s = jax.ShapeDtypeStruct(s1t.shape, F32)
    bf16s = jax.ShapeDtypeStruct((PEER_HEADS, n // LANES, PEER_NKEYS, LANES), BF16)
    tile_blk = pl.BlockSpec((1, tb // LANES, PEER_NKEYS, LANES), lambda h, i: (h, i, 0, 0))
    return pl.pallas_call(
        _peer_select_kernel,
        grid=(PEER_HEADS, n // tb),
        in_specs=[blk, blk],
        out_specs=[blk, blk, tile_blk, tile_blk],
        out_shape=[f32s, f32s, bf16s, bf16s],
        scratch_shapes=[pltpu.VMEM((PEER_TOPK, tb), F32), pltpu.VMEM((PEER_TOPK, tb), F32)],
        compiler_params=_cparams(("parallel", "parallel")),
        name="peer_select",
    )(s1t, s2t)


def _gelu_exact(x):
    return 0.5 * x * (1.0 + lax.erf(x * (2.0 ** -0.5)))


def _peer_dense_kernel(h2t_ref, u_ref, vt_ref, cnt_ref, p1_ref, rank_ref, p2_ref, out_ref,
                       acc_s, act_s, wg_s, cnt_s, p1_s, *, rows_per_step):
    e = pl.program_id(1)

    @pl.when(e == 0)
    def _():
        acc_s[...] = jnp.zeros_like(acc_s)

    act_s[...] = _dot(u_ref[...], h2t_ref[...])
    tb = act_s.shape[1]
    for h in range(PEER_HEADS):
        for r in range(rows_per_step):
            i = h * rows_per_step + r
            row_c = cnt_ref[h, pl.ds(e * rows_per_step + r, 1), :].astype(BF16)
            row_p = p1_ref[h, pl.ds(e * rows_per_step + r, 1), :].astype(BF16)
            for tt in range(tb // LANES):
                ts = slice(tt * LANES, (tt + 1) * LANES)
                cnt_s[tt, i] = jnp.broadcast_to(row_c[:, ts], (BF16_ROWS, LANES))
                p1_s[tt, i] = jnp.broadcast_to(row_p[:, ts], (BF16_ROWS, LANES))

    def token_tile(tt, carry):
        ts = pl.ds(pl.multiple_of(tt * LANES, LANES), LANES)
        for k in range(PEER_NKEYS // BF16_ROWS):
            ks = slice(k * BF16_ROWS, (k + 1) * BF16_ROWS)
            w = [jnp.zeros((BF16_ROWS, LANES), BF16) for _ in range(rows_per_step)]
            for h in range(PEER_HEADS):
                rank = rank_ref[h, tt, ks, :]
                p2 = p2_ref[h, tt, ks, :]
                for r in range(rows_per_step):
                    i = h * rows_per_step + r
                    on_p1 = jnp.minimum(jnp.maximum(cnt_s[tt, i] - rank, 0.0), p1_s[tt, i])
                    w[r] = w[r] + on_p1 * p2
            for r in range(rows_per_step):
                es = slice(r * PEER_NKEYS + k * BF16_ROWS, r * PEER_NKEYS + (k + 1) * BF16_ROWS)
                wg_s[es, ts] = w[r] * _gelu_exact(act_s[es, ts]).astype(BF16)
        return carry

    lax.fori_loop(0, tb // LANES, token_tile, 0)
    acc_s[...] += _dot(vt_ref[...], wg_s[...])

    @pl.when(e == pl.num_programs(1) - 1)
    def _():
        out_ref[...] = acc_s[...].T


def _peer_dense(h2t, u_bf, vt_bf, cnt, p1, rank, p2, tb, eb):
    n = h2t.shape[1]
    rows = eb // PEER_NKEYS
    rowblk = pl.BlockSpec((PEER_HEADS, PEER_NKEYS, tb), lambda i, e: (0, 0, i))
    tileblk = pl.BlockSpec((PEER_HEADS, tb // LANES, PEER_NKEYS, LANES), lambda i, e: (0, i, 0, 0))
    staged = pltpu.VMEM((tb // LANES, PEER_HEADS * rows, BF16_ROWS, LANES), BF16)
    return pl.pallas_call(
        functools.partial(_peer_dense_kernel, rows_per_step=rows),
        grid=(n // tb, PEER_EXPERTS // eb),
        in_specs=[pl.BlockSpec((D_MODEL, tb), lambda i, e: (0, i)),
                  pl.BlockSpec((eb, D_MODEL), lambda i, e: (e, 0)),
                  pl.BlockSpec((D_MODEL, eb), lambda i, e: (0, e)),
                  rowblk, rowblk, tileblk, tileblk],
        out_specs=pl.BlockSpec((tb, D_MODEL), lambda i, e: (i, 0)),
        out_shape=jax.ShapeDtypeStruct((n, D_MODEL), F32),
        scratch_shapes=[pltpu.VMEM((D_MODEL, tb), F32), pltpu.VMEM((eb, tb), F32), pltpu.VMEM((eb, tb), BF16),
                        staged, staged],
        compiler_params=_cparams(("parallel", "arbitrary")),
        name="peer_dense",
    )(h2t, u_bf, vt_bf, cnt, p1, rank, p2)


def _final_kernel(x1_ref, peer_ref, p_ref, wg_ref, wp_ref, g_ref, out_ref, *, normalize):
    x2 = x1_ref[...] + peer_ref[...]
    gate = jax.nn.sigmoid(_dot(x2.astype(BF16), wg_ref[...]))
    x3 = x2 + gate * _dot(p_ref[...].astype(BF16), wp_ref[...])
    out_ref[...] = _rms(x3, g_ref[...]) if normalize else x3


def _final(x1, peer, p2, w_gate, w_ple, g_final, tm, normalize):
    n = x1.shape[0]
    g2 = g_final[None, :]
    tok = lambda w: pl.BlockSpec((tm, w), lambda i: (i, 0))
    full = lambda a: pl.BlockSpec(a.shape, lambda i: (0,) * a.ndim)
    return pl.pallas_call(
        functools.partial(_final_kernel, normalize=normalize),
        grid=(n // tm,),
        in_specs=[tok(D_MODEL), tok(D_MODEL), tok(PLE_DIM), full(w_gate), full(w_ple), full(g2)],
        out_specs=tok(D_MODEL),
        out_shape=jax.ShapeDtypeStruct((n, D_MODEL), F32),
        compiler_params=_cparams(("parallel",)),
        name="ple_final_norm",
    )(x1, peer, p2, w_gate, w_ple, g2)


def _layer(x2, p2, posf, b, s, normalize, g_mix, w_in, g_kv, w_uk, w_uv, conv_w, conv_b, w_rg, b_rg, w_ig, b_ig,
           lam, w_out, g_ffn, w_pq, k1, k2, u_tab, v_tab, w_ple, w_ple_gate, g_final):
    n = b * s
    tm = min(512, n)
    tq = 128
    kc = 256
    w_all, wuk_bd = _pack_in_weights(w_in, w_uk)
    qcat, kvcat, iq, ik2, iw, xb, gate = _in_projection(x2, posf, g_mix, w_all, wuk_bd, g_kv, tm)

    r3 = lambda a: a.reshape(b, s, a.shape[-1])
    iw_rows = iw[:, :IDX_HEADS].reshape(b, s // tq, tq, IDX_HEADS).swapaxes(2, 3).reshape(b, s // tq, 1, IDX_HEADS * tq)
    kvt = r3(kvcat)[:, :, :KV_LATENT].reshape(b, s // kc, kc, KV_LATENT).swapaxes(2, 3)
    wuvt = jnp.swapaxes(w_uv, 1, 2).astype(BF16)
    att = _dsa_attention(r3(iq), iw_rows, qcat.reshape(ATT_HEADS, b, s, 256), r3(ik2), r3(kvcat), kvt, wuvt, tq, kc)
    rec = _rg_lru(r3(xb), r3(gate), conv_w, conv_b, w_rg, b_rg, w_ig, b_ig, lam, ts=min(256, s))

    x1, h2t, s1t, s2t = _out_projection(x2, att.reshape(n, 512), rec.reshape(n, 512), w_out.astype(BF16), g_ffn,
                                        w_pq.astype(BF16), k1.astype(BF16), k2.astype(BF16), tm)
    cnt, p1, rank, p2f = _peer_select(s1t, s2t, tb=min(256, n))
    peer = _peer_dense(h2t, u_tab.astype(BF16), v_tab.T.astype(BF16), cnt, p1, rank, p2f,
                       tb=min(512, n), eb=1024)
    return _final(x1, peer, p2, w_ple_gate.astype(BF16), w_ple.astype(BF16), g_final, tm, normalize)


def kernel(x, p, positions, g_mix, w_in, g_kv, w_uk, w_uv, conv_w, conv_b, w_rg, b_rg, w_ig, b_ig, lru_lambda,
           w_out, g_ffn, w_pq, peer_k1, peer_k2, peer_u, peer_v, w_ple, w_ple_gate, g_final):
    b, s, d = x.shape
    n = b * s
    depth = w_in.shape[0]
    posf = positions.astype(F32).reshape(n, 1)
    x2 = x.reshape(n, d)
    for i in range(depth):
        x2 = _layer(x2, p[i].reshape(n, PLE_DIM), posf, b, s, i == depth - 1, g_mix[i], w_in[i], g_kv[i], w_uk[i],
                    w_uv[i], conv_w[i], conv_b[i], w_rg[i], b_rg[i], w_ig[i], b_ig[i], lru_lambda[i], w_out[i],
                    g_ffn[i], w_pq[i], peer_k1[i], peer_k2[i], peer_u[i], peer_v[i], w_ple[i], w_ple_gate[i],
                    g_final)
    return x2.reshape(b, s, d)
```

```python
import functools

import jax
import jax.numpy as jnp
from jax import lax
from jax.experimental import pallas as pl
from jax.experimental.pallas import tpu as pltpu

D_MODEL = 1024
ATT_HEADS = 4
ATT_HEAD_DIM = 128
ROPE_DIM = 32
NOPE_DIM = ATT_HEAD_DIM - ROPE_DIM
KV_LATENT = 128
IDX_HEADS = 8
IDX_DIM = 64
IDX_ROPE_DIM = 16
IDX_TOPK_MAX = 256
LRU_WIDTH = 512
LRU_BLOCKS = 8
LRU_BLOCK_DIM = LRU_WIDTH // LRU_BLOCKS
CONV_WIDTH = 4
LRU_C = 8.0
ROPE_THETA = 500000.0
PLE_DIM = 256
PEER_HEADS = 8
PEER_NKEYS = 128
PEER_EXPERTS = PEER_NKEYS * PEER_NKEYS
PEER_HALF = 128
PEER_TOPK = 16
EPS = 1e-6

LANES = 128
SUBLANES = 8
BF16_ROWS = 16
GATE_BLOCK = 4
INT_MIN = -2147483648
VMEM_LIMIT = 56 * 1024 * 1024

F32 = jnp.float32
BF16 = jnp.bfloat16

_C_QN = 0
_C_QR = _C_QN + 384
_C_QRR = _C_QR + 128
_C_CKV = _C_QRR + 128
_C_KR = _C_CKV + 128
_C_KRR = _C_KR + 128
_C_IQ = _C_KRR + 128
_C_IQR = _C_IQ + 512
_C_IK = _C_IQR + 512
_C_IKR = _C_IK + 128
_C_IW = _C_IKR + 128
_C_XB = _C_IW + 128
_C_GATE = _C_XB + 512
_C_TOTAL = _C_GATE + 512


def _cparams(sem):
    return pltpu.CompilerParams(dimension_semantics=sem, vmem_limit_bytes=VMEM_LIMIT)


def _rms(x, g):
    return x * lax.rsqrt(jnp.mean(x * x, axis=-1, keepdims=True) + EPS) * g


def _dot(a, b):
    return jnp.dot(a, b, preferred_element_type=F32)


def _dot_nt(a, b):
    return lax.dot_general(a, b, (((1,), (1,)), ((), ())), preferred_element_type=F32)


def _rot_half(w, half):
    return jnp.concatenate([-w[:, half:2 * half], w[:, :half]], axis=1)


def _pack_in_weights(w_in, w_uk):
    d = w_in.shape[0]
    o = 0
    wq = w_in[:, o:o + 512]; o += 512
    wckv = w_in[:, o:o + 128]; o += 128
    wkr = w_in[:, o:o + 32]; o += 32
    wiq = w_in[:, o:o + 512]; o += 512
    wik = w_in[:, o:o + 64]; o += 64
    wiw = w_in[:, o:o + 8]; o += 8
    wxb = w_in[:, o:o + 512]; o += 512
    wgate = w_in[:, o:o + 512]

    qn, qr, qrr = [], [], []
    for h in range(ATT_HEADS):
        wh = wq[:, h * 128:(h + 1) * 128]
        qr.append(wh[:, :ROPE_DIM])
        qrr.append(_rot_half(wh[:, :ROPE_DIM], ROPE_DIM // 2))
        qn.append(wh[:, ROPE_DIM:])
    iqr = []
    zpad = jnp.zeros((d, IDX_DIM - IDX_ROPE_DIM), w_in.dtype)
    for h in range(IDX_HEADS):
        wh = wiq[:, h * 64:(h + 1) * 64]
        iqr.append(jnp.concatenate([_rot_half(wh[:, :IDX_ROPE_DIM], IDX_ROPE_DIM // 2), zpad], axis=1))
    ikr = jnp.concatenate([_rot_half(wik[:, :IDX_ROPE_DIM], IDX_ROPE_DIM // 2), zpad], axis=1)
    cols = [
        jnp.concatenate(qn, axis=1),
        jnp.concatenate(qr, axis=1),
        jnp.concatenate(qrr, axis=1),
        wckv,
        jnp.tile(wkr, (1, 4)),
        jnp.tile(_rot_half(wkr, ROPE_DIM // 2), (1, 4)),
        wiq,
        jnp.concatenate(iqr, axis=1),
        jnp.tile(wik, (1, 2)),
        jnp.tile(ikr, (1, 2)),
        jnp.concatenate([wiw, jnp.zeros((d, 120), w_in.dtype)], axis=1),
        wxb,
        wgate,
    ]
    w_all = jnp.concatenate(cols, axis=1).astype(BF16)
    assert w_all.shape[1] == _C_TOTAL
    wuk_bd = jnp.zeros((ATT_HEADS * NOPE_DIM, ATT_HEADS * KV_LATENT), F32)
    for h in range(ATT_HEADS):
        wuk_bd = wuk_bd.at[h * NOPE_DIM:(h + 1) * NOPE_DIM, h * KV_LATENT:(h + 1) * KV_LATENT].set(w_uk[h].T)
    return w_all, wuk_bd.astype(BF16)


def _rope_freq_rows():
    inv_a = ROPE_THETA ** (-jnp.arange(0, ROPE_DIM, 2, dtype=F32) / ROPE_DIM)
    inv_i = ROPE_THETA ** (-jnp.arange(0, IDX_ROPE_DIM, 2, dtype=F32) / IDX_ROPE_DIM)
    row_a = jnp.tile(inv_a, LANES // inv_a.shape[0])[None, :]
    head_i = jnp.concatenate([inv_i, inv_i, jnp.zeros((IDX_DIM - IDX_ROPE_DIM,), F32)])
    row_i = jnp.tile(head_i, LANES // IDX_DIM)[None, :]
    return row_a, row_i


def _inproj_kernel(x_ref, pos_ref, g_ref, fa_ref, fi_ref, w_ref, wuk_ref, gkv_ref,
                   qcat_ref, kvcat_ref, iq_ref, ik_ref, iw_ref, xb_ref, gate_ref):
    hb = _rms(x_ref[...], g_ref[...]).astype(BF16)
    pos = pos_ref[...]

    def proj(c0, width):
        return _dot(hb, w_ref[:, c0:c0 + width])

    ang_a = pos * fa_ref[...]
    cos_a, sin_a = jnp.cos(ang_a), jnp.sin(ang_a)
    ang_i = pos * fi_ref[...]
    cos_i, sin_i = jnp.cos(ang_i), jnp.sin(ang_i)

    q_lat = _dot(proj(_C_QN, 384).astype(BF16), wuk_ref[...])
    q_rope = proj(_C_QR, 128) * cos_a + proj(_C_QRR, 128) * sin_a
    lane = lax.broadcasted_iota(jnp.int32, q_rope.shape, 1)
    for h in range(ATT_HEADS):
        qcat_ref[h, :, 0:128] = q_lat[:, h * 128:(h + 1) * 128].astype(BF16)
        keep = (lane >> 5) == h
        qcat_ref[h, :, 128:256] = jnp.where(keep, q_rope, 0.0).astype(BF16)

    kvcat_ref[:, 0:128] = _rms(proj(_C_CKV, 128), gkv_ref[...]).astype(BF16)
    kvcat_ref[:, 128:256] = (proj(_C_KR, 128) * cos_a + proj(_C_KRR, 128) * sin_a).astype(BF16)

    for c in range(4):
        sl = slice(c * 128, (c + 1) * 128)
        iq_ref[:, sl] = (proj(_C_IQ + c * 128, 128) * cos_i + proj(_C_IQR + c * 128, 128) * sin_i).astype(BF16)
    ik_ref[...] = (proj(_C_IK, 128) * cos_i + proj(_C_IKR, 128) * sin_i).astype(BF16)
    iw_ref[...] = proj(_C_IW, 128) * (IDX_HEADS ** -0.5 * IDX_DIM ** -0.5)
    xb_ref[...] = proj(_C_XB, 512)
    gate_ref[...] = proj(_C_GATE, 512)


def _in_projection(x2, posf, g_mix, w_all, wuk_bd, g_kv, tm):
    n = x2.shape[0]
    row_a, row_i = _rope_freq_rows()
    tok = lambda w: pl.BlockSpec((tm, w), lambda i: (i, 0))
    full = lambda a: pl.BlockSpec(a.shape, lambda i: (0,) * a.ndim)
    outs = [
        jax.ShapeDtypeStruct((ATT_HEADS, n, 256), BF16),
        jax.ShapeDtypeStruct((n, 256), BF16),
        jax.ShapeDtypeStruct((n, 512), BF16),
        jax.ShapeDtypeStruct((n, 128), BF16),
        jax.ShapeDtypeStruct((n, 128), F32),
        jax.ShapeDtypeStruct((n, 512), F32),
        jax.ShapeDtypeStruct((n, 512), F32),
    ]
    g2 = g_mix[None, :]
    gkv2 = g_kv[None, :]
    return pl.pallas_call(
        _inproj_kernel,
        grid=(n // tm,),
        in_specs=[tok(D_MODEL), tok(1), full(g2), full(row_a), full(row_i), full(w_all), full(wuk_bd), full(gkv2)],
        out_specs=[pl.BlockSpec((ATT_HEADS, tm, 256), lambda i: (0, i, 0))] + [tok(s.shape[1]) for s in outs[1:]],
        out_shape=outs,
        compiler_params=_cparams(("parallel",)),
        name="in_projection",
    )(x2, posf, g2, row_a, row_i, w_all, wuk_bd, gkv2)


def _ordered_bits_to_float(u):
    s = u ^ INT_MIN
    return pltpu.bitcast(s ^ ((s >> 31) & jnp.int32(0x7FFFFFFF)), F32)


def _tree(op, xs):
    while len(xs) > 1:
        xs = [op(xs[i], xs[i + 1]) if i + 1 < len(xs) else xs[i] for i in range(0, len(xs), 2)]
    return xs[0]


def _col_sum8(x):
    return jnp.sum(x.reshape(x.shape[0] // SUBLANES, SUBLANES, x.shape[1]), axis=0)


def _dsa_kernel(iq_ref, iw_ref, qcat_ref, ik_ref, kvcat_ref, kvt_ref, wuvt_ref, out_ref,
                iqm_s, score_s, lg_s, o_s, fv_s, fp_s, *, seq, tq, kc, topk):
    j = pl.program_id(1)
    t0 = j * tq
    nk = (t0 + tq + kc - 1) // kc
    tpos = t0 + lax.broadcasted_iota(jnp.int32, (kc, tq), 1)
    row = lax.broadcasted_iota(jnp.int32, (kc, tq), 0)

    lane = lax.broadcasted_iota(jnp.int32, (tq, LANES), 1)
    for h in range(IDX_HEADS):
        pair = iq_ref[0, :, (h // 2) * 128:(h // 2 + 1) * 128]
        keep = (lane < IDX_DIM) if h % 2 == 0 else (lane >= IDX_DIM)
        iqm_s[h * tq:(h + 1) * tq, :] = jnp.where(keep, pair, jnp.zeros_like(pair))
    iw_row = iw_ref[0, 0]

    def score_chunk(c, carry):
        k0 = pl.multiple_of(c * kc, kc)
        r = jnp.maximum(_dot_nt(ik_ref[0, pl.ds(k0, kc), :], iqm_s[...]), 0.0) * iw_row
        sc = r[:, 0:tq]
        for h in range(1, IDX_HEADS):
            sc = sc + r[:, h * tq:(h + 1) * tq]
        score_s[pl.ds(k0, kc), :] = jnp.where(k0 + row <= tpos, sc, -jnp.inf)
        return carry

    lax.fori_loop(0, nk, score_chunk, 0)

    def fold(fn, op, reduce_rows, init):
        def body(c, acc):
            k0 = pl.multiple_of(c * kc, kc)
            v = fn(score_s[pl.ds(k0, kc), :], k0)
            return op(acc, _tree(op, [v[g * SUBLANES:(g + 1) * SUBLANES] for g in range(kc // SUBLANES)]))
        acc = lax.fori_loop(0, nk, body, jnp.full((SUBLANES, tq), init, jnp.asarray(init).dtype))
        return reduce_rows(acc, axis=0, keepdims=True)

    def count(pred):
        return fold(lambda blk, k0: pred(blk, k0).astype(jnp.int32), jnp.add, jnp.sum, jnp.int32(0))

    def bit_pass(i, t_u):
        cand_u = t_u | jnp.left_shift(jnp.int32(1), 31 - i)
        cand = _ordered_bits_to_float(cand_u)
        return jnp.where(count(lambda blk, k0: blk >= cand) >= topk, cand_u, t_u)

    t_u = lax.fori_loop(0, 32, bit_pass, jnp.zeros((1, tq), jnp.int32))
    t_u = jnp.where((t_u >> 23) == 0, jnp.int32(0x00800000), t_u)
    thr = _ordered_bits_to_float(t_u)

    n_ge = count(lambda blk, k0: blk >= thr)
    fv_s[...] = thr
    fp_s[...] = jnp.full((1, tq), seq, jnp.int32)
    extra = jnp.max(n_ge) - topk

    @pl.when(extra > 0)
    def _():
        def drop_one(i, carry):
            fv, fp, kept = carry

            def kept_scores(blk, k0):
                pos = k0 + row
                return jnp.where(blk == fv, jnp.where(pos < fp, blk, jnp.inf), jnp.where(blk > fv, blk, jnp.inf))

            low = fold(kept_scores, jnp.minimum, jnp.min, jnp.float32(jnp.inf))

            def kept_pos_at_low(blk, k0):
                pos = k0 + row
                kept_pos = jnp.where(blk == fv, jnp.where(pos < fp, pos, -1), jnp.where(blk > fv, pos, -1))
                return jnp.where(blk == low, kept_pos, -1)

            last = fold(kept_pos_at_low, jnp.maximum, jnp.max, jnp.int32(-1))
            over = kept > topk
            return jnp.where(over, low, fv), jnp.where(over, last, fp), jnp.where(over, kept - 1, kept)

        fv, fp, _ = lax.fori_loop(0, extra, drop_one, (thr, jnp.full((1, tq), seq, jnp.int32), n_ge))
        fv_s[...] = fv
        fp_s[...] = fp

    floor_v = fv_s[...]
    floor_p = fp_s[...]
    qcat = qcat_ref[:, 0].reshape(ATT_HEADS * tq, 256)
    scale = ATT_HEAD_DIM ** -0.5

    def logit_chunk(c, m8):
        k0 = pl.multiple_of(c * kc, kc)
        blk = score_s[pl.ds(k0, kc), :]
        sel = jnp.where(blk == floor_v, (k0 + row < floor_p).astype(jnp.int32), (blk > floor_v).astype(jnp.int32))
        bias = jnp.where(sel > 0, 0.0, -jnp.inf)
        lg = _dot_nt(kvcat_ref[0, pl.ds(k0, kc), :], qcat) * scale
        lg = lg + jnp.concatenate([bias] * ATT_HEADS, axis=1)
        lg_s[pl.ds(k0, kc), :] = lg
        return jnp.maximum(m8, jnp.max(lg.reshape(kc // SUBLANES, SUBLANES, ATT_HEADS * tq), axis=0))

    m8 = lax.fori_loop(0, nk, logit_chunk, jnp.full((SUBLANES, ATT_HEADS * tq), -jnp.inf, F32))
    m = jnp.max(m8, axis=0, keepdims=True)
    o_s[...] = jnp.zeros_like(o_s)

    def pv_chunk(c, den8):
        k0 = pl.multiple_of(c * kc, kc)
        p = jnp.exp(lg_s[pl.ds(k0, kc), :] - m)
        o_s[...] += _dot(kvt_ref[0, c], p.astype(BF16))
        return den8 + _col_sum8(p)

    den8 = lax.fori_loop(0, nk, pv_chunk, jnp.zeros((SUBLANES, ATT_HEADS * tq), F32))
    o = (o_s[...] / jnp.sum(den8, axis=0, keepdims=True)).astype(BF16)
    outs = [_dot(wuvt_ref[h], o[:, h * tq:(h + 1) * tq]) for h in range(ATT_HEADS)]
    out_ref[0] = jnp.concatenate(outs, axis=0).T.astype(BF16)


def _dsa_attention(iq, iw_rows, qcat, ik2, kvcat, kvt, wuvt, tq, kc):
    b, s, _ = iq.shape
    topk = min(IDX_TOPK_MAX, s // 4)
    assert s & (s - 1) == 0, "sequence length must be a power of two"
    assert kc >= topk and s % kc == 0 and kc % tq == 0
    qb = lambda w: pl.BlockSpec((1, tq, w), lambda i, j: (i, j, 0))
    kb = lambda w: pl.BlockSpec((1, s, w), lambda i, j: (i, 0, 0))
    return pl.pallas_call(
        functools.partial(_dsa_kernel, seq=s, tq=tq, kc=kc, topk=topk),
        grid=(b, s // tq),
        in_specs=[qb(512),
                  pl.BlockSpec((1, 1, 1, IDX_HEADS * tq), lambda i, j: (i, j, 0, 0)),
                  pl.BlockSpec((ATT_HEADS, 1, tq, 256), lambda i, j: (0, i, j, 0)),
                  kb(128), kb(256),
                  pl.BlockSpec((1, s // kc, KV_LATENT, kc), lambda i, j: (i, 0, 0, 0)),
                  pl.BlockSpec(wuvt.shape, lambda i, j: (0, 0, 0))],
        out_specs=qb(512),
        out_shape=jax.ShapeDtypeStruct((b, s, 512), BF16),
        scratch_shapes=[pltpu.VMEM((IDX_HEADS * tq, LANES), BF16),
                        pltpu.VMEM((s, tq), F32),
                        pltpu.VMEM((s, ATT_HEADS * tq), F32),
                        pltpu.VMEM((KV_LATENT, ATT_HEADS * tq), F32),
                        pltpu.VMEM((1, tq), F32), pltpu.VMEM((1, tq), jnp.int32)],
        compiler_params=_cparams(("parallel", "parallel")),
        name="dsa_attention",
    )(iq, iw_rows, qcat, ik2, kvcat, kvt, wuvt)


def _lru_kernel(xb_ref, gate_ref, cw_ref, cb_ref, wrg_ref, brg_ref, wig_ref, big_ref, lam_ref, out_ref,
                xpad_s, h_s, *, ts):
    @pl.when(pl.program_id(1) == 0)
    def _():
        xpad_s[0:8, :] = jnp.zeros((8, LRU_WIDTH), F32)
        h_s[...] = jnp.zeros_like(h_s)

    xb = xb_ref[0]
    xpad_s[8:8 + ts, :] = xb
    xc = cb_ref[...]
    for w in range(CONV_WIDTH):
        off = 8 - (CONV_WIDTH - 1) + w
        xc = xc + cw_ref[w:w + 1, :] * xpad_s[off:off + ts, :]
    xpad_s[0:8, :] = xb[ts - 8:ts, :]

    xcb = xc.astype(BF16)
    r = jax.nn.sigmoid(_dot(xcb, wrg_ref[...]) + brg_ref[...])
    i = jax.nn.sigmoid(_dot(xcb, wig_ref[...]) + big_ref[...])
    log_a = -LRU_C * r * jax.nn.softplus(-lam_ref[...])
    a = jnp.exp(log_a)
    bv = jnp.sqrt(1.0 - jnp.exp(2.0 * log_a)) * (i * xc)

    row = lax.broadcasted_iota(jnp.int32, (ts, LRU_WIDTH), 0)
    d = 1
    while d < ts:
        a_sh = jnp.where(row >= d, pltpu.roll(a, d, 0), 1.0)
        b_sh = jnp.where(row >= d, pltpu.roll(bv, d, 0), 0.0)
        bv = a * b_sh + bv
        a = a * a_sh
        d *= 2
    h = a * h_s[...] + bv
    h_s[...] = h[ts - 1:ts, :]
    out_ref[0] = (h * jax.nn.gelu(gate_ref[0])).astype(BF16)


def _block_diag(w):
    nb, bi, bo = w.shape
    out = jnp.zeros((nb * bi, nb * bo), w.dtype)
    for k in range(nb):
        out = out.at[k * bi:(k + 1) * bi, k * bo:(k + 1) * bo].set(w[k])
    return out


def _rg_lru(xb, gate, conv_w, conv_b, w_rg, b_rg, w_ig, b_ig, lam, ts):
    b, s, c = xb.shape
    row = lambda a: a[None, :]
    args = [conv_w, row(conv_b), _block_diag(w_rg).astype(BF16), row(b_rg),
            _block_diag(w_ig).astype(BF16), row(b_ig), row(lam)]
    tb = pl.BlockSpec((1, ts, c), lambda i, j: (i, j, 0))
    full = lambda a: pl.BlockSpec(a.shape, lambda i, j: (0,) * a.ndim)
    return pl.pallas_call(
        functools.partial(_lru_kernel, ts=ts),
        grid=(b, s // ts),
        in_specs=[tb, tb] + [full(a) for a in args],
        out_specs=tb,
        out_shape=jax.ShapeDtypeStruct((b, s, c), BF16),
        scratch_shapes=[pltpu.VMEM((ts + 8, c), F32), pltpu.VMEM((1, c), F32)],
        compiler_params=_cparams(("parallel", "arbitrary")),
        name="rg_lru",
    )(xb, gate, *args)


def _outproj_kernel(x_ref, att_ref, rec_ref, wo_ref, g_ref, wpq_ref, k1_ref, k2_ref,
                    x1_ref, h2t_ref, s1_ref, s2_ref):
    x1 = x_ref[...] + _dot(att_ref[...], wo_ref[0:512, :]) + _dot(rec_ref[...], wo_ref[512:1024, :])
    x1_ref[...] = x1
    h2f = _rms(x1, g_ref[...])
    h2t_ref[...] = h2f.T.astype(BF16)
    h2 = h2f.astype(BF16)
    for h in range(PEER_HEADS):
        q = _dot(h2, wpq_ref[:, h * 256:(h + 1) * 256]).astype(BF16)
        s1_ref[h] = _dot_nt(k1_ref[h], q[:, :PEER_HALF])
        s2_ref[h] = _dot_nt(k2_ref[h], q[:, PEER_HALF:])


def _out_projection(x2, att, rec, w_out, g_ffn, w_pq, k1, k2, tm):
    n = x2.shape[0]
    g2 = g_ffn[None, :]
    tok = lambda w: pl.BlockSpec((tm, w), lambda i: (i, 0))
    full = lambda a: pl.BlockSpec(a.shape, lambda i: (0,) * a.ndim)
    sblk = pl.BlockSpec((PEER_HEADS, PEER_NKEYS, tm), lambda i: (0, 0, i))
    s_shape = jax.ShapeDtypeStruct((PEER_HEADS, PEER_NKEYS, n), F32)
    return pl.pallas_call(
        _outproj_kernel,
        grid=(n // tm,),
        in_specs=[tok(D_MODEL), tok(512), tok(512), full(w_out), full(g2), full(w_pq), full(k1), full(k2)],
        out_specs=[tok(D_MODEL), pl.BlockSpec((D_MODEL, tm), lambda i: (0, i)), sblk, sblk],
        out_shape=[jax.ShapeDtypeStruct((n, D_MODEL), F32), jax.ShapeDtypeStruct((D_MODEL, n), BF16),
                   s_shape, s_shape],
        compiler_params=_cparams(("parallel",)),
        name="out_projection",
    )(x2, att, rec, w_out, g2, w_pq, k1, k2)


def _top_values(s, k):
    vals = []
    for _ in range(k):
        m = jnp.max(s, axis=0, keepdims=True)
        vals.append(m)
        s = jnp.where(s == m, -jnp.inf, s)
    return vals


def _kth_largest(s, k):
    kth = None
    above = jnp.zeros((1, s.shape[1]), F32)
    for _ in range(k):
        m = jnp.max(s, axis=0, keepdims=True)
        hit = s == m
        kth = m if kth is None else jnp.where(above < k, m, kth)
        above = above + jnp.sum(jnp.where(hit, 1.0, 0.0), axis=0, keepdims=True)
        s = jnp.where(hit, -jnp.inf, s)
    return kth


def _peer_select_kernel(s1_ref, s2_ref, cnt_ref, p1_ref, rank_ref, p2_ref, v1_s, v2_s):
    s1 = s1_ref[0]
    s2 = s2_ref[0]
    t = s1.shape[1]
    v1 = _top_values(s1, PEER_TOPK)
    for i in range(PEER_TOPK):
        v1_s[i:i + 1, :] = v1[i]
    v1_all = v1_s[...]
    v2 = []
    cur = s2
    rank = jnp.full(s2.shape, float(PEER_TOPK), F32)
    for j in range(PEER_TOPK):
        m = jnp.max(cur, axis=0, keepdims=True)
        v2.append(m)
        hit = cur == m
        rank = jnp.where(hit, float(j), rank)
        cur = jnp.where(hit, -jnp.inf, cur)
        v2_s[j:j + 1, :] = m
    v2_all = v2_s[...]

    m1, m2 = v1[0], v2[0]
    e1_all = jnp.exp(v1_all - m1)
    e2_all = jnp.exp(v2_all - m2)
    row8 = lax.broadcasted_iota(jnp.int32, (SUBLANES, t), 0)
    slabs = [(v1[0] + v2_all[0:8], e1_all[0:1] * e2_all[0:8]),
             (v1[0] + v2_all[8:16], e1_all[0:1] * e2_all[8:16])]
    for i in range(1, 8):
        keep = row8 < PEER_TOPK // (i + 1)
        slabs.append((jnp.where(keep, v1[i] + v2_all[0:8], -jnp.inf), e1_all[i:i + 1] * e2_all[0:8]))
    slabs.append((v1_all[8:16] + v2[0], e1_all[8:16] * e2_all[0:1]))
    cand = jnp.concatenate([c for c, _ in slabs], axis=0)
    thr = _kth_largest(cand, PEER_TOPK)
    z = jnp.sum(jnp.where(cand >= thr, jnp.concatenate([e for _, e in slabs], axis=0), 0.0), axis=0, keepdims=True)

    cnt = jnp.zeros(s1.shape, F32)
    for j in range(PEER_TOPK):
        cnt = cnt + jnp.where(s1 + v2[j] >= thr, 1.0, 0.0)
    cnt_ref[0] = cnt
    p1_ref[0] = jnp.exp(s1 - m1) / z
    p2 = jnp.exp(s2 - m2)
    for c in range(t // LANES):
        rank_ref[0, c] = rank[:, c * LANES:(c + 1) * LANES]
        p2_ref[0, c] = p2[:, c * LANES:(c + 1) * LANES]


def _peer_select(s1t, s2t, tb):
    n = s1t.shape[2]
    blk = pl.BlockSpec((1, PEER_NKEYS, tb), lambda h, i: (h, 0, i))
    f32s = jax.ShapeDtypeStruct(s1t.shape, F32)
    tiles = jax.ShapeDtypeStruct((PEER_HEADS, n // LANES, PEER_NKEYS, LANES), F32)
    tile_blk = pl.BlockSpec((1, tb // LANES, PEER_NKEYS, LANES), lambda h, i: (h, i, 0, 0))
    return pl.pallas_call(
        _peer_select_kernel,
        grid=(PEER_HEADS, n // tb),
        in_specs=[blk, blk],
        out_specs=[blk, blk, tile_blk, tile_blk],
        out_shape=[f32s, f32s, tiles, tiles],
        scratch_shapes=[pltpu.VMEM((PEER_TOPK, tb), F32), pltpu.VMEM((PEER_TOPK, tb), F32)],
        compiler_params=_cparams(("parallel", "parallel")),
        name="peer_select",
    )(s1t, s2t)


def _gelu_exact(x):
    return 0.5 * x * (1.0 + lax.erf(x * (2.0 ** -0.5)))


def _peer_dense_kernel(h2t_ref, u_ref, vt_ref, cnt_ref, p1_ref, rank_ref, p2_ref, out_ref,
                       acc_s, act_s, wg_s, cnt_s, p1_s, *, rows_per_step):
    e = pl.program_id(1)

    @pl.when(e == 0)
    def _():
        acc_s[...] = jnp.zeros_like(acc_s)

    act = _dot(u_ref[...], h2t_ref[...])
    n_tiles = act_s.shape[0]
    tb = n_tiles * LANES
    for tt in range(n_tiles):
        act_s[tt] = act[:, tt * LANES:(tt + 1) * LANES]
    for h in range(PEER_HEADS):
        for r in range(rows_per_step):
            i = h * rows_per_step + r
            row_c = cnt_ref[h, pl.ds(e * rows_per_step + r, 1), :]
            row_p = p1_ref[h, pl.ds(e * rows_per_step + r, 1), :]
            for tt in range(tb // LANES):
                ts = slice(tt * LANES, (tt + 1) * LANES)
                cnt_s[tt, i] = jnp.broadcast_to(row_c[:, ts], (SUBLANES, LANES))
                p1_s[tt, i] = jnp.broadcast_to(row_p[:, ts], (SUBLANES, LANES))

    def token_tile(tt, carry):
        key_tiles = PEER_NKEYS // SUBLANES
        for k0 in range(0, key_tiles, GATE_BLOCK):
            for r0 in range(0, rows_per_step, GATE_BLOCK):
                kk = range(k0, k0 + GATE_BLOCK)
                rr = range(r0, r0 + GATE_BLOCK)
                w = {(k, r): jnp.zeros((SUBLANES, LANES), F32) for k in kk for r in rr}
                for h in range(PEER_HEADS):
                    rank = {k: rank_ref[h, tt, k * SUBLANES:(k + 1) * SUBLANES, :] for k in kk}
                    p2 = {k: p2_ref[h, tt, k * SUBLANES:(k + 1) * SUBLANES, :] for k in kk}
                    for r in rr:
                        cnt = cnt_s[tt, h * rows_per_step + r]
                        p1 = p1_s[tt, h * rows_per_step + r]
                        for k in kk:
                            w[k, r] = w[k, r] + p1 * jnp.where(rank[k] < cnt, p2[k], 0.0)
                for r in rr:
                    for k in range(k0, k0 + GATE_BLOCK, 2):
                        es = slice(r * PEER_NKEYS + k * SUBLANES, r * PEER_NKEYS + (k + 2) * SUBLANES)
                        w2 = jnp.concatenate([w[k, r], w[k + 1, r]], axis=0)
                        wg_s[tt, es, :] = (w2 * _gelu_exact(act_s[tt, es, :])).astype(BF16)
        return carry

    lax.fori_loop(0, n_tiles, token_tile, 0)
    wg = jnp.concatenate([wg_s[tt] for tt in range(n_tiles)], axis=1)
    acc_s[...] += _dot(vt_ref[...], wg)

    @pl.when(e == pl.num_programs(1) - 1)
    def _():
        out_ref[...] = acc_s[...].T


def _peer_dense(h2t, u_bf, vt_bf, cnt, p1, rank, p2, tb, eb):
    n = h2t.shape[1]
    rows = eb // PEER_NKEYS
    rowblk = pl.BlockSpec((PEER_HEADS, PEER_NKEYS, tb), lambda i, e: (0, 0, i))
    tileblk = pl.BlockSpec((PEER_HEADS, tb // LANES, PEER_NKEYS, LANES), lambda i, e: (0, i, 0, 0))
    staged = pltpu.VMEM((tb // LANES, PEER_HEADS * rows, SUBLANES, LANES), F32)
    return pl.pallas_call(
        functools.partial(_peer_dense_kernel, rows_per_step=rows),
        grid=(n // tb, PEER_EXPERTS // eb),
        in_specs=[pl.BlockSpec((D_MODEL, tb), lambda i, e: (0, i)),
                  pl.BlockSpec((eb, D_MODEL), lambda i, e: (e, 0)),
                  pl.BlockSpec((D_MODEL, eb), lambda i, e: (0, e)),
                  rowblk, rowblk, tileblk, tileblk],
        out_specs=pl.BlockSpec((tb, D_MODEL), lambda i, e: (i, 0)),
        out_shape=jax.ShapeDtypeStruct((n, D_MODEL), F32),
        scratch_shapes=[pltpu.VMEM((D_MODEL, tb), F32), pltpu.VMEM((tb // LANES, eb, LANES), F32),
                        pltpu.VMEM((tb // LANES, eb, LANES), BF16), staged, staged],
        compiler_params=_cparams(("parallel", "arbitrary")),
        name="peer_dense",
    )(h2t, u_bf, vt_bf, cnt, p1, rank, p2)


def _final_kernel(x1_ref, peer_ref, p_ref, wg_ref, wp_ref, g_ref, out_ref, *, normalize):
    x2 = x1_ref[...] + peer_ref[...]
    gate = jax.nn.sigmoid(_dot(x2.astype(BF16), wg_ref[...]))
    x3 = x2 + gate * _dot(p_ref[...].astype(BF16), wp_ref[...])
    out_ref[...] = _rms(x3, g_ref[...]) if normalize else x3


def _final(x1, peer, p2, w_gate, w_ple, g_final, tm, normalize):
    n = x1.shape[0]
    g2 = g_final[None, :]
    tok = lambda w: pl.BlockSpec((tm, w), lambda i: (i, 0))
    full = lambda a: pl.BlockSpec(a.shape, lambda i: (0,) * a.ndim)
    return pl.pallas_call(
        functools.partial(_final_kernel, normalize=normalize),
        grid=(n // tm,),
        in_specs=[tok(D_MODEL), tok(D_MODEL), tok(PLE_DIM), full(w_gate), full(w_ple), full(g2)],
        out_specs=tok(D_MODEL),
        out_shape=jax.ShapeDtypeStruct((n, D_MODEL), F32),
        compiler_params=_cparams(("parallel",)),
        name="ple_final_norm",
    )(x1, peer, p2, w_gate, w_ple, g2)


def _layer(x2, p2, posf, b, s, normalize, g_mix, w_in, g_kv, w_uk, w_uv, conv_w, conv_b, w_rg, b_rg, w_ig, b_ig,
           lam, w_out, g_ffn, w_pq, k1, k2, u_tab, v_tab, w_ple, w_ple_gate, g_final):
    n = b * s
    tm = min(512, n)
    tq = 128
    kc = 256
    w_all, wuk_bd = _pack_in_weights(w_in, w_uk)
    qcat, kvcat, iq, ik2, iw, xb, gate = _in_projection(x2, posf, g_mix, w_all, wuk_bd, g_kv, tm)

    r3 = lambda a: a.reshape(b, s, a.shape[-1])
    iw_rows = iw[:, :IDX_HEADS].reshape(b, s // tq, tq, IDX_HEADS).swapaxes(2, 3).reshape(b, s // tq, 1, IDX_HEADS * tq)
    kvt = r3(kvcat)[:, :, :KV_LATENT].reshape(b, s // kc, kc, KV_LATENT).swapaxes(2, 3)
    wuvt = jnp.swapaxes(w_uv, 1, 2).astype(BF16)
    att = _dsa_attention(r3(iq), iw_rows, qcat.reshape(ATT_HEADS, b, s, 256), r3(ik2), r3(kvcat), kvt, wuvt, tq, kc)
    rec = _rg_lru(r3(xb), r3(gate), conv_w, conv_b, w_rg, b_rg, w_ig, b_ig, lam, ts=min(256, s))

    x1, h2t, s1t, s2t = _out_projection(x2, att.reshape(n, 512), rec.reshape(n, 512), w_out.astype(BF16), g_ffn,
                                        w_pq.astype(BF16), k1.astype(BF16), k2.astype(BF16), tm)
    cnt, p1, rank, p2f = _peer_select(s1t, s2t, tb=min(256, n))
    peer = _peer_dense(h2t, u_tab.astype(BF16), v_tab.T.astype(BF16), cnt, p1, rank, p2f,
                       tb=min(512, n), eb=2048)
    return _final(x1, peer, p2, w_ple_gate.astype(BF16), w_ple.astype(BF16), g_final, tm, normalize)


def kernel(x, p, positions, g_mix, w_in, g_kv, w_uk, w_uv, conv_w, conv_b, w_rg, b_rg, w_ig, b_ig, lru_lambda,
           w_out, g_ffn, w_pq, peer_k1, peer_k2, peer_u, peer_v, w_ple, w_ple_gate, g_final):
    b, s, d = x.shape
    n = b * s
    depth = w_in.shape[0]
    posf = positions.astype(F32).reshape(n, 1)
    x2 = x.reshape(n, d)
    for i in range(depth):
        x2 = _layer(x2, p[i].reshape(n, PLE_DIM), posf, b, s, i == depth - 1, g_mix[i], w_in[i], g_kv[i], w_uk[i],
                    w_uv[i], conv_w[i], conv_b[i], w_rg[i], b_rg[i], w_ig[i], b_ig[i], lru_lambda[i], w_out[i],
                    g_ffn[i], w_pq[i], peer_k1[i], peer_k2[i], peer_u[i], peer_v[i], w_ple[i], w_ple_gate[i],
                    g_final)
    return x2.reshape(b, s, d)
```

```python
import functools

import jax
import jax.numpy as jnp
from jax import lax
from jax.experimental import pallas as pl
from jax.experimental.pallas import tpu as pltpu

D_MODEL = 1024
ATT_HEADS = 4
ATT_HEAD_DIM = 128
ROPE_DIM = 32
NOPE_DIM = ATT_HEAD_DIM - ROPE_DIM
KV_LATENT = 128
IDX_HEADS = 8
IDX_DIM = 64
IDX_ROPE_DIM = 16
IDX_TOPK_MAX = 256
LRU_WIDTH = 512
LRU_BLOCKS = 8
LRU_BLOCK_DIM = LRU_WIDTH // LRU_BLOCKS
CONV_WIDTH = 4
LRU_C = 8.0
ROPE_THETA = 500000.0
PLE_DIM = 256
PEER_HEADS = 8
PEER_NKEYS = 128
PEER_EXPERTS = PEER_NKEYS * PEER_NKEYS
PEER_HALF = 128
PEER_TOPK = 16
EPS = 1e-6

LANES = 128
SUBLANES = 8
BF16_ROWS = 16
GATE_BLOCK = 4
INT_MIN = -2147483648
VMEM_LIMIT = 56 * 1024 * 1024

F32 = jnp.float32
BF16 = jnp.bfloat16

_C_QN = 0
_C_QR = _C_QN + 384
_C_QRR = _C_QR + 128
_C_CKV = _C_QRR + 128
_C_KR = _C_CKV + 128
_C_KRR = _C_KR + 128
_C_IQ = _C_KRR + 128
_C_IQR = _C_IQ + 512
_C_IK = _C_IQR + 512
_C_IKR = _C_IK + 128
_C_IW = _C_IKR + 128
_C_XB = _C_IW + 128
_C_GATE = _C_XB + 512
_C_TOTAL = _C_GATE + 512


def _cparams(sem):
    return pltpu.CompilerParams(dimension_semantics=sem, vmem_limit_bytes=VMEM_LIMIT)


def _rms(x, g):
    return x * lax.rsqrt(jnp.mean(x * x, axis=-1, keepdims=True) + EPS) * g


def _dot(a, b):
    return jnp.dot(a, b, preferred_element_type=F32)


def _dot_nt(a, b):
    return lax.dot_general(a, b, (((1,), (1,)), ((), ())), preferred_element_type=F32)


def _rot_half(w, half):
    return jnp.concatenate([-w[:, half:2 * half], w[:, :half]], axis=1)


def _pack_in_weights(w_in, w_uk):
    d = w_in.shape[0]
    o = 0
    wq = w_in[:, o:o + 512]; o += 512
    wckv = w_in[:, o:o + 128]; o += 128
    wkr = w_in[:, o:o + 32]; o += 32
    wiq = w_in[:, o:o + 512]; o += 512
    wik = w_in[:, o:o + 64]; o += 64
    wiw = w_in[:, o:o + 8]; o += 8
    wxb = w_in[:, o:o + 512]; o += 512
    wgate = w_in[:, o:o + 512]

    qn, qr, qrr = [], [], []
    for h in range(ATT_HEADS):
        wh = wq[:, h * 128:(h + 1) * 128]
        qr.append(wh[:, :ROPE_DIM])
        qrr.append(_rot_half(wh[:, :ROPE_DIM], ROPE_DIM // 2))
        qn.append(wh[:, ROPE_DIM:])
    iqr = []
    zpad = jnp.zeros((d, IDX_DIM - IDX_ROPE_DIM), w_in.dtype)
    for h in range(IDX_HEADS):
        wh = wiq[:, h * 64:(h + 1) * 64]
        iqr.append(jnp.concatenate([_rot_half(wh[:, :IDX_ROPE_DIM], IDX_ROPE_DIM // 2), zpad], axis=1))
    ikr = jnp.concatenate([_rot_half(wik[:, :IDX_ROPE_DIM], IDX_ROPE_DIM // 2), zpad], axis=1)
    cols = [
        jnp.concatenate(qn, axis=1),
        jnp.concatenate(qr, axis=1),
        jnp.concatenate(qrr, axis=1),
        wckv,
        jnp.tile(wkr, (1, 4)),
        jnp.tile(_rot_half(wkr, ROPE_DIM // 2), (1, 4)),
        wiq,
        jnp.concatenate(iqr, axis=1),
        jnp.tile(wik, (1, 2)),
        jnp.tile(ikr, (1, 2)),
        jnp.concatenate([wiw, jnp.zeros((d, 120), w_in.dtype)], axis=1),
        wxb,
        wgate,
    ]
    w_all = jnp.concatenate(cols, axis=1).astype(BF16)
    assert w_all.shape[1] == _C_TOTAL
    wuk_bd = jnp.zeros((ATT_HEADS * NOPE_DIM, ATT_HEADS * KV_LATENT), F32)
    for h in range(ATT_HEADS):
        wuk_bd = wuk_bd.at[h * NOPE_DIM:(h + 1) * NOPE_DIM, h * KV_LATENT:(h + 1) * KV_LATENT].set(w_uk[h].T)
    return w_all, wuk_bd.astype(BF16)


def _rope_freq_rows():
    inv_a = ROPE_THETA ** (-jnp.arange(0, ROPE_DIM, 2, dtype=F32) / ROPE_DIM)
    inv_i = ROPE_THETA ** (-jnp.arange(0, IDX_ROPE_DIM, 2, dtype=F32) / IDX_ROPE_DIM)
    row_a = jnp.tile(inv_a, LANES // inv_a.shape[0])[None, :]
    head_i = jnp.concatenate([inv_i, inv_i, jnp.zeros((IDX_DIM - IDX_ROPE_DIM,), F32)])
    row_i = jnp.tile(head_i, LANES // IDX_DIM)[None, :]
    return row_a, row_i


def _inproj_kernel(x_ref, pos_ref, g_ref, fa_ref, fi_ref, w_ref, wuk_ref, gkv_ref,
                   qcat_ref, kvcat_ref, iq_ref, ik_ref, iw_ref, xb_ref, gate_ref):
    hb = _rms(x_ref[...], g_ref[...]).astype(BF16)
    pos = pos_ref[...]

    def proj(c0, width):
        return _dot(hb, w_ref[:, c0:c0 + width])

    ang_a = pos * fa_ref[...]
    cos_a, sin_a = jnp.cos(ang_a), jnp.sin(ang_a)
    ang_i = pos * fi_ref[...]
    cos_i, sin_i = jnp.cos(ang_i), jnp.sin(ang_i)

    q_lat = _dot(proj(_C_QN, 384).astype(BF16), wuk_ref[...])
    q_rope = proj(_C_QR, 128) * cos_a + proj(_C_QRR, 128) * sin_a
    lane = lax.broadcasted_iota(jnp.int32, q_rope.shape, 1)
    for h in range(ATT_HEADS):
        qcat_ref[h, :, 0:128] = q_lat[:, h * 128:(h + 1) * 128].astype(BF16)
        keep = (lane >> 5) == h
        qcat_ref[h, :, 128:256] = jnp.where(keep, q_rope, 0.0).astype(BF16)

    kvcat_ref[:, 0:128] = _rms(proj(_C_CKV, 128), gkv_ref[...]).astype(BF16)
    kvcat_ref[:, 128:256] = (proj(_C_KR, 128) * cos_a + proj(_C_KRR, 128) * sin_a).astype(BF16)

    for c in range(4):
        sl = slice(c * 128, (c + 1) * 128)
        iq_ref[:, sl] = (proj(_C_IQ + c * 128, 128) * cos_i + proj(_C_IQR + c * 128, 128) * sin_i).astype(BF16)
    ik_ref[...] = (proj(_C_IK, 128) * cos_i + proj(_C_IKR, 128) * sin_i).astype(BF16)
    iw_ref[...] = proj(_C_IW, 128) * (IDX_HEADS ** -0.5 * IDX_DIM ** -0.5)
    xb_ref[...] = proj(_C_XB, 512)
    gate_ref[...] = proj(_C_GATE, 512)


def _in_projection(x2, posf, g_mix, w_all, wuk_bd, g_kv, tm):
    n = x2.shape[0]
    row_a, row_i = _rope_freq_rows()
    tok = lambda w: pl.BlockSpec((tm, w), lambda i: (i, 0))
    full = lambda a: pl.BlockSpec(a.shape, lambda i: (0,) * a.ndim)
    outs = [
        jax.ShapeDtypeStruct((ATT_HEADS, n, 256), BF16),
        jax.ShapeDtypeStruct((n, 256), BF16),
        jax.ShapeDtypeStruct((n, 512), BF16),
        jax.ShapeDtypeStruct((n, 128), BF16),
        jax.ShapeDtypeStruct((n, 128), F32),
        jax.ShapeDtypeStruct((n, 512), F32),
        jax.ShapeDtypeStruct((n, 512), F32),
    ]
    g2 = g_mix[None, :]
    gkv2 = g_kv[None, :]
    return pl.pallas_call(
        _inproj_kernel,
        grid=(n // tm,),
        in_specs=[tok(D_MODEL), tok(1), full(g2), full(row_a), full(row_i), full(w_all), full(wuk_bd), full(gkv2)],
        out_specs=[pl.BlockSpec((ATT_HEADS, tm, 256), lambda i: (0, i, 0))] + [tok(s.shape[1]) for s in outs[1:]],
        out_shape=outs,
        compiler_params=_cparams(("parallel",)),
        name="in_projection",
    )(x2, posf, g2, row_a, row_i, w_all, wuk_bd, gkv2)


def _ordered_bits_to_float(u):
    s = u ^ INT_MIN
    return pltpu.bitcast(s ^ ((s >> 31) & jnp.int32(0x7FFFFFFF)), F32)


def _tree(op, xs):
    while len(xs) > 1:
        xs = [op(xs[i], xs[i + 1]) if i + 1 < len(xs) else xs[i] for i in range(0, len(xs), 2)]
    return xs[0]


def _col_sum8(x):
    return jnp.sum(x.reshape(x.shape[0] // SUBLANES, SUBLANES, x.shape[1]), axis=0)


def _dsa_kernel(iq_ref, iw_ref, qcat_ref, ik_ref, kvcat_ref, kvt_ref, wuvt_ref, out_ref,
                iqm_s, score_s, lg_s, o_s, fv_s, fp_s, *, seq, tq, kc, topk, nb):
    j = pl.program_id(1)
    t0 = j * tq
    nk = (t0 + tq + kc - 1) // kc
    tpos = t0 + lax.broadcasted_iota(jnp.int32, (kc, tq), 1)
    row = lax.broadcasted_iota(jnp.int32, (kc, tq), 0)
    batch = range(nb)

    lane = lax.broadcasted_iota(jnp.int32, (tq, LANES), 1)
    for g in batch:
        for h in range(IDX_HEADS):
            pair = iq_ref[g, :, (h // 2) * 128:(h // 2 + 1) * 128]
            keep = (lane < IDX_DIM) if h % 2 == 0 else (lane >= IDX_DIM)
            iqm_s[g, h * tq:(h + 1) * tq, :] = jnp.where(keep, pair, jnp.zeros_like(pair))

    def score_chunk(c, carry):
        k0 = pl.multiple_of(c * kc, kc)
        for g in batch:
            r = jnp.maximum(_dot_nt(ik_ref[g, pl.ds(k0, kc), :], iqm_s[g]), 0.0) * iw_ref[g, 0]
            sc = r[:, 0:tq]
            for h in range(1, IDX_HEADS):
                sc = sc + r[:, h * tq:(h + 1) * tq]
            score_s[g, pl.ds(k0, kc), :] = jnp.where(k0 + row <= tpos, sc, -jnp.inf)
        return carry

    lax.fori_loop(0, nk, score_chunk, 0)

    def fold(fn, op, reduce_rows, init):
        def body(c, accs):
            k0 = pl.multiple_of(c * kc, kc)
            out = []
            for g in batch:
                v = fn(g, score_s[g, pl.ds(k0, kc), :], k0)
                out.append(op(accs[g], _tree(op, [v[i * SUBLANES:(i + 1) * SUBLANES] for i in range(kc // SUBLANES)])))
            return tuple(out)
        init_acc = jnp.full((SUBLANES, tq), init, jnp.asarray(init).dtype)
        accs = lax.fori_loop(0, nk, body, tuple(init_acc for _ in batch))
        return [reduce_rows(a, axis=0, keepdims=True) for a in accs]

    def count(pred):
        return fold(lambda g, blk, k0: pred(g, blk, k0).astype(jnp.int32), jnp.add, jnp.sum, jnp.int32(0))

    def bit_pass(i, t_us):
        cand_us = [t_u | jnp.left_shift(jnp.int32(1), 31 - i) for t_u in t_us]
        cands = [_ordered_bits_to_float(c) for c in cand_us]
        cnts = count(lambda g, blk, k0: blk >= cands[g])
        return tuple(jnp.where(cnts[g] >= topk, cand_us[g], t_us[g]) for g in batch)

    t_us = lax.fori_loop(0, 32, bit_pass, tuple(jnp.zeros((1, tq), jnp.int32) for _ in batch))
    thr = [_ordered_bits_to_float(jnp.where((t_u >> 23) == 0, jnp.int32(0x00800000), t_u)) for t_u in t_us]

    n_ge = count(lambda g, blk, k0: blk >= thr[g])
    for g in batch:
        fv_s[g] = thr[g]
        fp_s[g] = jnp.full((1, tq), seq, jnp.int32)
    extra = _tree(jnp.maximum, [jnp.max(n) for n in n_ge]) - topk

    @pl.when(extra > 0)
    def _():
        def drop_one(i, carry):
            fv, fp, kept = carry

            def kept_scores(g, blk, k0):
                pos = k0 + row
                return jnp.where(blk == fv[g], jnp.where(pos < fp[g], blk, jnp.inf), jnp.where(blk > fv[g], blk, jnp.inf))

            low = fold(kept_scores, jnp.minimum, jnp.min, jnp.float32(jnp.inf))

            def kept_pos_at_low(g, blk, k0):
                pos = k0 + row
                kept_pos = jnp.where(blk == fv[g], jnp.where(pos < fp[g], pos, -1), jnp.where(blk > fv[g], pos, -1))
                return jnp.where(blk == low[g], kept_pos, -1)

            last = fold(kept_pos_at_low, jnp.maximum, jnp.max, jnp.int32(-1))
            over = [kept[g] > topk for g in batch]
            return (tuple(jnp.where(over[g], low[g], fv[g]) for g in batch),
                    tuple(jnp.where(over[g], last[g], fp[g]) for g in batch),
                    tuple(jnp.where(over[g], kept[g] - 1, kept[g]) for g in batch))

        start = (tuple(thr), tuple(jnp.full((1, tq), seq, jnp.int32) for _ in batch), tuple(n_ge))
        fv, fp, _ = lax.fori_loop(0, extra, drop_one, start)
        for g in batch:
            fv_s[g] = fv[g]
            fp_s[g] = fp[g]

    floor_v = [fv_s[g] for g in batch]
    floor_p = [fp_s[g] for g in batch]
    qcat = [qcat_ref[:, g].reshape(ATT_HEADS * tq, 256) for g in batch]
    scale = ATT_HEAD_DIM ** -0.5

    def logit_chunk(c, m8s):
        k0 = pl.multiple_of(c * kc, kc)
        out = []
        for g in batch:
            blk = score_s[g, pl.ds(k0, kc), :]
            sel = jnp.where(blk == floor_v[g], (k0 + row < floor_p[g]).astype(jnp.int32),
                            (blk > floor_v[g]).astype(jnp.int32))
            bias = jnp.where(sel > 0, 0.0, -jnp.inf)
            lg = _dot_nt(kvcat_ref[g, pl.ds(k0, kc), :], qcat[g]) * scale
            lg = lg + jnp.concatenate([bias] * ATT_HEADS, axis=1)
            lg_s[g, pl.ds(k0, kc), :] = lg
            out.append(jnp.maximum(m8s[g], jnp.max(lg.reshape(kc // SUBLANES, SUBLANES, ATT_HEADS * tq), axis=0)))
        return tuple(out)

    m8s = lax.fori_loop(0, nk, logit_chunk,
                        tuple(jnp.full((SUBLANES, ATT_HEADS * tq), -jnp.inf, F32) for _ in batch))
    m = [jnp.max(m8, axis=0, keepdims=True) for m8 in m8s]
    o_s[...] = jnp.zeros_like(o_s)

    def pv_chunk(c, den8s):
        k0 = pl.multiple_of(c * kc, kc)
        out = []
        for g in batch:
            p = jnp.exp(lg_s[g, pl.ds(k0, kc), :] - m[g])
            o_s[g] += _dot(kvt_ref[g, c], p.astype(BF16))
            out.append(den8s[g] + _col_sum8(p))
        return tuple(out)

    den8s = lax.fori_loop(0, nk, pv_chunk, tuple(jnp.zeros((SUBLANES, ATT_HEADS * tq), F32) for _ in batch))
    for g in batch:
        o = (o_s[g] / jnp.sum(den8s[g], axis=0, keepdims=True)).astype(BF16)
        outs = [_dot(wuvt_ref[h], o[:, h * tq:(h + 1) * tq]) for h in range(ATT_HEADS)]
        out_ref[g] = jnp.concatenate(outs, axis=0).T.astype(BF16)


def _dsa_attention(iq, iw_rows, qcat, ik2, kvcat, kvt, wuvt, tq, kc, nb):
    b, s, _ = iq.shape
    topk = min(IDX_TOPK_MAX, s // 4)
    assert s & (s - 1) == 0, "sequence length must be a power of two"
    assert kc >= topk and s % kc == 0 and s % tq == 0 and b % nb == 0
    qb = lambda w: pl.BlockSpec((nb, tq, w), lambda i, j: (i, j, 0))
    kb = lambda w: pl.BlockSpec((nb, s, w), lambda i, j: (i, 0, 0))
    per_batch = lambda shape, dt: pltpu.VMEM((nb,) + shape, dt)
    return pl.pallas_call(
        functools.partial(_dsa_kernel, seq=s, tq=tq, kc=kc, topk=topk, nb=nb),
        grid=(b // nb, s // tq),
        in_specs=[qb(512),
                  pl.BlockSpec((nb, 1, 1, IDX_HEADS * tq), lambda i, j: (i, j, 0, 0)),
                  pl.BlockSpec((ATT_HEADS, nb, tq, 256), lambda i, j: (0, i, j, 0)),
                  kb(128), kb(256),
                  pl.BlockSpec((nb, s // kc, KV_LATENT, kc), lambda i, j: (i, 0, 0, 0)),
                  pl.BlockSpec(wuvt.shape, lambda i, j: (0, 0, 0))],
        out_specs=qb(512),
        out_shape=jax.ShapeDtypeStruct((b, s, 512), BF16),
        scratch_shapes=[per_batch((IDX_HEADS * tq, LANES), BF16),
                        per_batch((s, tq), F32),
                        per_batch((s, ATT_HEADS * tq), F32),
                        per_batch((KV_LATENT, ATT_HEADS * tq), F32),
                        per_batch((1, tq), F32), per_batch((1, tq), jnp.int32)],
        compiler_params=_cparams(("parallel", "parallel")),
        name="dsa_attention",
    )(iq, iw_rows, qcat, ik2, kvcat, kvt, wuvt)


def _lru_kernel(xb_ref, gate_ref, cw_ref, cb_ref, wrg_ref, brg_ref, wig_ref, big_ref, lam_ref, out_ref,
                xpad_s, h_s, *, ts):
    @pl.when(pl.program_id(1) == 0)
    def _():
        xpad_s[0:8, :] = jnp.zeros((8, LRU_WIDTH), F32)
        h_s[...] = jnp.zeros_like(h_s)

    xb = xb_ref[0]
    xpad_s[8:8 + ts, :] = xb
    xc = cb_ref[...]
    for w in range(CONV_WIDTH):
        off = 8 - (CONV_WIDTH - 1) + w
        xc = xc + cw_ref[w:w + 1, :] * xpad_s[off:off + ts, :]
    xpad_s[0:8, :] = xb[ts - 8:ts, :]

    xcb = xc.astype(BF16)
    r = jax.nn.sigmoid(_dot(xcb, wrg_ref[...]) + brg_ref[...])
    i = jax.nn.sigmoid(_dot(xcb, wig_ref[...]) + big_ref[...])
    log_a = -LRU_C * r * jax.nn.softplus(-lam_ref[...])
    a = jnp.exp(log_a)
    bv = jnp.sqrt(1.0 - jnp.exp(2.0 * log_a)) * (i * xc)

    row = lax.broadcasted_iota(jnp.int32, (ts, LRU_WIDTH), 0)
    d = 1
    while d < ts:
        a_sh = jnp.where(row >= d, pltpu.roll(a, d, 0), 1.0)
        b_sh = jnp.where(row >= d, pltpu.roll(bv, d, 0), 0.0)
        bv = a * b_sh + bv
        a = a * a_sh
        d *= 2
    h = a * h_s[...] + bv
    h_s[...] = h[ts - 1:ts, :]
    out_ref[0] = (h * jax.nn.gelu(gate_ref[0])).astype(BF16)


def _block_diag(w):
    nb, bi, bo = w.shape
    out = jnp.zeros((nb * bi, nb * bo), w.dtype)
    for k in range(nb):
        out = out.at[k * bi:(k + 1) * bi, k * bo:(k + 1) * bo].set(w[k])
    return out


def _rg_lru(xb, gate, conv_w, conv_b, w_rg, b_rg, w_ig, b_ig, lam, ts):
    b, s, c = xb.shape
    row = lambda a: a[None, :]
    args = [conv_w, row(conv_b), _block_diag(w_rg).astype(BF16), row(b_rg),
            _block_diag(w_ig).astype(BF16), row(b_ig), row(lam)]
    tb = pl.BlockSpec((1, ts, c), lambda i, j: (i, j, 0))
    full = lambda a: pl.BlockSpec(a.shape, lambda i, j: (0,) * a.ndim)
    return pl.pallas_call(
        functools.partial(_lru_kernel, ts=ts),
        grid=(b, s // ts),
        in_specs=[tb, tb] + [full(a) for a in args],
        out_specs=tb,
        out_shape=jax.ShapeDtypeStruct((b, s, c), BF16),
        scratch_shapes=[pltpu.VMEM((ts + 8, c), F32), pltpu.VMEM((1, c), F32)],
        compiler_params=_cparams(("parallel", "arbitrary")),
        name="rg_lru",
    )(xb, gate, *args)


def _outproj_kernel(x_ref, att_ref, rec_ref, wo_ref, g_ref, wpq_ref, k1_ref, k2_ref,
                    x1_ref, h2t_ref, s1_ref, s2_ref):
    x1 = x_ref[...] + _dot(att_ref[...], wo_ref[0:512, :]) + _dot(rec_ref[...], wo_ref[512:1024, :])
    x1_ref[...] = x1
    h2f = _rms(x1, g_ref[...])
    h2t_ref[...] = h2f.T.astype(BF16)
    h2 = h2f.astype(BF16)
    for h in range(PEER_HEADS):
        q = _dot(h2, wpq_ref[:, h * 256:(h + 1) * 256]).astype(BF16)
        s1_ref[h] = _dot_nt(k1_ref[h], q[:, :PEER_HALF])
        s2_ref[h] = _dot_nt(k2_ref[h], q[:, PEER_HALF:])


def _out_projection(x2, att, rec, w_out, g_ffn, w_pq, k1, k2, tm):
    n = x2.shape[0]
    g2 = g_ffn[None, :]
    tok = lambda w: pl.BlockSpec((tm, w), lambda i: (i, 0))
    full = lambda a: pl.BlockSpec(a.shape, lambda i: (0,) * a.ndim)
    sblk = pl.BlockSpec((PEER_HEADS, PEER_NKEYS, tm), lambda i: (0, 0, i))
    s_shape = jax.ShapeDtypeStruct((PEER_HEADS, PEER_NKEYS, n), F32)
    return pl.pallas_call(
        _outproj_kernel,
        grid=(n // tm,),
        in_specs=[tok(D_MODEL), tok(512), tok(512), full(w_out), full(g2), full(w_pq), full(k1), full(k2)],
        out_specs=[tok(D_MODEL), pl.BlockSpec((D_MODEL, tm), lambda i: (0, i)), sblk, sblk],
        out_shape=[jax.ShapeDtypeStruct((n, D_MODEL), F32), jax.ShapeDtypeStruct((D_MODEL, n), BF16),
                   s_shape, s_shape],
        compiler_params=_cparams(("parallel",)),
        name="out_projection",
    )(x2, att, rec, w_out, g2, w_pq, k1, k2)


def _top_values(s, k):
    vals = []
    for _ in range(k):
        m = jnp.max(s, axis=0, keepdims=True)
        vals.append(m)
        s = jnp.where(s == m, -jnp.inf, s)
    return vals


def _kth_largest(s, k):
    kth = None
    above = jnp.zeros((1, s.shape[1]), F32)
    for _ in range(k):
        m = jnp.max(s, axis=0, keepdims=True)
        hit = s == m
        kth = m if kth is None else jnp.where(above < k, m, kth)
        above = above + jnp.sum(jnp.where(hit, 1.0, 0.0), axis=0, keepdims=True)
        s = jnp.where(hit, -jnp.inf, s)
    return kth


def _peer_select_kernel(s1_ref, s2_ref, cnt_ref, p1_ref, rank_ref, p2_ref, v1_s, v2_s):
    s1 = s1_ref[0]
    s2 = s2_ref[0]
    t = s1.shape[1]
    v1 = _top_values(s1, PEER_TOPK)
    for i in range(PEER_TOPK):
        v1_s[i:i + 1, :] = v1[i]
    v1_all = v1_s[...]
    v2 = []
    cur = s2
    rank = jnp.full(s2.shape, float(PEER_TOPK), F32)
    for j in range(PEER_TOPK):
        m = jnp.max(cur, axis=0, keepdims=True)
        v2.append(m)
        hit = cur == m
        rank = jnp.where(hit, float(j), rank)
        cur = jnp.where(hit, -jnp.inf, cur)
        v2_s[j:j + 1, :] = m
    v2_all = v2_s[...]

    m1, m2 = v1[0], v2[0]
    e1_all = jnp.exp(v1_all - m1)
    e2_all = jnp.exp(v2_all - m2)
    row8 = lax.broadcasted_iota(jnp.int32, (SUBLANES, t), 0)
    slabs = [(v1[0] + v2_all[0:8], e1_all[0:1] * e2_all[0:8]),
             (v1[0] + v2_all[8:16], e1_all[0:1] * e2_all[8:16])]
    for i in range(1, 8):
        keep = row8 < PEER_TOPK // (i + 1)
        slabs.append((jnp.where(keep, v1[i] + v2_all[0:8], -jnp.inf), e1_all[i:i + 1] * e2_all[0:8]))
    slabs.append((v1_all[8:16] + v2[0], e1_all[8:16] * e2_all[0:1]))
    cand = jnp.concatenate([c for c, _ in slabs], axis=0)
    thr = _kth_largest(cand, PEER_TOPK)
    z = jnp.sum(jnp.where(cand >= thr, jnp.concatenate([e for _, e in slabs], axis=0), 0.0), axis=0, keepdims=True)

    cnt = jnp.zeros(s1.shape, F32)
    for j in range(PEER_TOPK):
        cnt = cnt + jnp.where(s1 + v2[j] >= thr, 1.0, 0.0)
    cnt_ref[0] = cnt
    p1_ref[0] = jnp.exp(s1 - m1) / z
    p2 = jnp.exp(s2 - m2)
    for c in range(t // LANES):
        rank_ref[0, c] = rank[:, c * LANES:(c + 1) * LANES]
        p2_ref[0, c] = p2[:, c * LANES:(c + 1) * LANES]


def _peer_select(s1t, s2t, tb):
    n = s1t.shape[2]
    blk = pl.BlockSpec((1, PEER_NKEYS, tb), lambda h, i: (h, 0, i))
    f32s = jax.ShapeDtypeStruct(s1t.shape, F32)
    tiles = jax.ShapeDtypeStruct((PEER_HEADS, n // LANES, PEER_NKEYS, LANES), F32)
    tile_blk = pl.BlockSpec((1, tb // LANES, PEER_NKEYS, LANES), lambda h, i: (h, i, 0, 0))
    return pl.pallas_call(
        _peer_select_kernel,
        grid=(PEER_HEADS, n // tb),
        in_specs=[blk, blk],
        out_specs=[blk, blk, tile_blk, tile_blk],
        out_shape=[f32s, f32s, tiles, tiles],
        scratch_shapes=[pltpu.VMEM((PEER_TOPK, tb), F32), pltpu.VMEM((PEER_TOPK, tb), F32)],
        compiler_params=_cparams(("parallel", "parallel")),
        name="peer_select",
    )(s1t, s2t)


def _gelu_exact(x):
    return 0.5 * x * (1.0 + lax.erf(x * (2.0 ** -0.5)))


def _peer_dense_kernel(h2t_ref, u_ref, vt_ref, cnt_ref, p1_ref, rank_ref, p2_ref, out_ref,
                       acc_s, act_s, wg_s, cnt_s, p1_s, *, rows_per_step):
    e = pl.program_id(1)

    @pl.when(e == 0)
    def _():
        acc_s[...] = jnp.zeros_like(acc_s)

    act = _dot(u_ref[...], h2t_ref[...])
    n_tiles = act_s.shape[0]
    tb = n_tiles * LANES
    for tt in range(n_tiles):
        act_s[tt] = act[:, tt * LANES:(tt + 1) * LANES]
    for h in range(PEER_HEADS):
        for r in range(rows_per_step):
            i = h * rows_per_step + r
            row_c = cnt_ref[h, pl.ds(e * rows_per_step + r, 1), :]
            row_p = p1_ref[h, pl.ds(e * rows_per_step + r, 1), :]
            for tt in range(tb // LANES):
                ts = slice(tt * LANES, (tt + 1) * LANES)
                cnt_s[tt, i] = jnp.broadcast_to(row_c[:, ts], (SUBLANES, LANES))
                p1_s[tt, i] = jnp.broadcast_to(row_p[:, ts], (SUBLANES, LANES))

    def token_tile(tt, carry):
        key_tiles = PEER_NKEYS // SUBLANES
        for k0 in range(0, key_tiles, GATE_BLOCK):
            for r0 in range(0, rows_per_step, GATE_BLOCK):
                kk = range(k0, k0 + GATE_BLOCK)
                rr = range(r0, r0 + GATE_BLOCK)
                w = {(k, r): jnp.zeros((SUBLANES, LANES), F32) for k in kk for r in rr}
                for h in range(PEER_HEADS):
                    rank = {k: rank_ref[h, tt, k * SUBLANES:(k + 1) * SUBLANES, :] for k in kk}
                    p2 = {k: p2_ref[h, tt, k * SUBLANES:(k + 1) * SUBLANES, :] for k in kk}
                    for r in rr:
                        cnt = cnt_s[tt, h * rows_per_step + r]
                        p1 = p1_s[tt, h * rows_per_step + r]
                        for k in kk:
                            w[k, r] = w[k, r] + p1 * jnp.where(rank[k] < cnt, p2[k], 0.0)
                for r in rr:
                    for k in range(k0, k0 + GATE_BLOCK, 2):
                        es = slice(r * PEER_NKEYS + k * SUBLANES, r * PEER_NKEYS + (k + 2) * SUBLANES)
                        w2 = jnp.concatenate([w[k, r], w[k + 1, r]], axis=0)
                        wg_s[tt, es, :] = (w2 * _gelu_exact(act_s[tt, es, :])).astype(BF16)
        return carry

    lax.fori_loop(0, n_tiles, token_tile, 0)
    wg = jnp.concatenate([wg_s[tt] for tt in range(n_tiles)], axis=1)
    acc_s[...] += _dot(vt_ref[...], wg)

    @pl.when(e == pl.num_programs(1) - 1)
    def _():
        out_ref[...] = acc_s[...].T


def _peer_dense(h2t, u_bf, vt_bf, cnt, p1, rank, p2, tb, eb):
    n = h2t.shape[1]
    rows = eb // PEER_NKEYS
    rowblk = pl.BlockSpec((PEER_HEADS, PEER_NKEYS, tb), lambda i, e: (0, 0, i))
    tileblk = pl.BlockSpec((PEER_HEADS, tb // LANES, PEER_NKEYS, LANES), lambda i, e: (0, i, 0, 0))
    staged = pltpu.VMEM((tb // LANES, PEER_HEADS * rows, SUBLANES, LANES), F32)
    return pl.pallas_call(
        functools.partial(_peer_dense_kernel, rows_per_step=rows),
        grid=(n // tb, PEER_EXPERTS // eb),
        in_specs=[pl.BlockSpec((D_MODEL, tb), lambda i, e: (0, i)),
                  pl.BlockSpec((eb, D_MODEL), lambda i, e: (e, 0)),
                  pl.BlockSpec((D_MODEL, eb), lambda i, e: (0, e)),
                  rowblk, rowblk, tileblk, tileblk],
        out_specs=pl.BlockSpec((tb, D_MODEL), lambda i, e: (i, 0)),
        out_shape=jax.ShapeDtypeStruct((n, D_MODEL), F32),
        scratch_shapes=[pltpu.VMEM((D_MODEL, tb), F32), pltpu.VMEM((tb // LANES, eb, LANES), F32),
                        pltpu.VMEM((tb // LANES, eb, LANES), BF16), staged, staged],
        compiler_params=_cparams(("parallel", "arbitrary")),
        name="peer_dense",
    )(h2t, u_bf, vt_bf, cnt, p1, rank, p2)


def _final_kernel(x1_ref, peer_ref, p_ref, wg_ref, wp_ref, g_ref, out_ref, *, normalize):
    x2 = x1_ref[...] + peer_ref[...]
    gate = jax.nn.sigmoid(_dot(x2.astype(BF16), wg_ref[...]))
    x3 = x2 + gate * _dot(p_ref[...].astype(BF16), wp_ref[...])
    out_ref[...] = _rms(x3, g_ref[...]) if normalize else x3


def _final(x1, peer, p2, w_gate, w_ple, g_final, tm, normalize):
    n = x1.shape[0]
    g2 = g_final[None, :]
    tok = lambda w: pl.BlockSpec((tm, w), lambda i: (i, 0))
    full = lambda a: pl.BlockSpec(a.shape, lambda i: (0,) * a.ndim)
    return pl.pallas_call(
        functools.partial(_final_kernel, normalize=normalize),
        grid=(n // tm,),
        in_specs=[tok(D_MODEL), tok(D_MODEL), tok(PLE_DIM), full(w_gate), full(w_ple), full(g2)],
        out_specs=tok(D_MODEL),
        out_shape=jax.ShapeDtypeStruct((n, D_MODEL), F32),
        compiler_params=_cparams(("parallel",)),
        name="ple_final_norm",
    )(x1, peer, p2, w_gate, w_ple, g2)


def _layer(x2, p2, posf, b, s, normalize, g_mix, w_in, g_kv, w_uk, w_uv, conv_w, conv_b, w_rg, b_rg, w_ig, b_ig,
           lam, w_out, g_ffn, w_pq, k1, k2, u_tab, v_tab, w_ple, w_ple_gate, g_final):
    n = b * s
    tm = min(512, n)
    tq = 128
    kc = 256
    w_all, wuk_bd = _pack_in_weights(w_in, w_uk)
    qcat, kvcat, iq, ik2, iw, xb, gate = _in_projection(x2, posf, g_mix, w_all, wuk_bd, g_kv, tm)

    r3 = lambda a: a.reshape(b, s, a.shape[-1])
    iw_rows = iw[:, :IDX_HEADS].reshape(b, s // tq, tq, IDX_HEADS).swapaxes(2, 3).reshape(b, s // tq, 1, IDX_HEADS * tq)
    kvt = r3(kvcat)[:, :, :KV_LATENT].reshape(b, s // kc, kc, KV_LATENT).swapaxes(2, 3)
    wuvt = jnp.swapaxes(w_uv, 1, 2).astype(BF16)
    att = _dsa_attention(r3(iq), iw_rows, qcat.reshape(ATT_HEADS, b, s, 256), r3(ik2), r3(kvcat), kvt, wuvt, tq, kc, nb=4 if b % 4 == 0 else 1)
    rec = _rg_lru(r3(xb), r3(gate), conv_w, conv_b, w_rg, b_rg, w_ig, b_ig, lam, ts=min(256, s))

    x1, h2t, s1t, s2t = _out_projection(x2, att.reshape(n, 512), rec.reshape(n, 512), w_out.astype(BF16), g_ffn,
                                        w_pq.astype(BF16), k1.astype(BF16), k2.astype(BF16), tm)
    cnt, p1, rank, p2f = _peer_select(s1t, s2t, tb=min(512, n))
    peer = _peer_dense(h2t, u_tab.astype(BF16), v_tab.T.astype(BF16), cnt, p1, rank, p2f,
                       tb=min(512, n), eb=2048)
    return _final(x1, peer, p2, w_ple_gate.astype(BF16), w_ple.astype(BF16), g_final, tm, normalize)


def kernel(x, p, positions, g_mix, w_in, g_kv, w_uk, w_uv, conv_w, conv_b, w_rg, b_rg, w_ig, b_ig, lru_lambda,
           w_out, g_ffn, w_pq, peer_k1, peer_k2, peer_u, peer_v, w_ple, w_ple_gate, g_final):
    b, s, d = x.shape
    n = b * s
    depth = w_in.shape[0]
    posf = positions.astype(F32).reshape(n, 1)
    x2 = x.reshape(n, d)
    for i in range(depth):
        x2 = _layer(x2, p[i].reshape(n, PLE_DIM), posf, b, s, i == depth - 1, g_mix[i], w_in[i], g_kv[i], w_uk[i],
                    w_uv[i], conv_w[i], conv_b[i], w_rg[i], b_rg[i], w_ig[i], b_ig[i], lru_lambda[i], w_out[i],
                    g_ffn[i], w_pq[i], peer_k1[i], peer_k2[i], peer_u[i], peer_v[i], w_ple[i], w_ple_gate[i],
                    g_final)
    return x2.reshape(b, s, d)
```

```python
import functools

import jax
import jax.numpy as jnp
from jax import lax
from jax.experimental import pallas as pl
from jax.experimental.pallas import tpu as pltpu

D_MODEL = 1024
ATT_HEADS = 4
ATT_HEAD_DIM = 128
ROPE_DIM = 32
NOPE_DIM = ATT_HEAD_DIM - ROPE_DIM
KV_LATENT = 128
IDX_HEADS = 8
IDX_DIM = 64
IDX_ROPE_DIM = 16
IDX_TOPK_MAX = 256
LRU_WIDTH = 512
LRU_BLOCKS = 8
LRU_BLOCK_DIM = LRU_WIDTH // LRU_BLOCKS
CONV_WIDTH = 4
LRU_C = 8.0
ROPE_THETA = 500000.0
PLE_DIM = 256
PEER_HEADS = 8
PEER_NKEYS = 128
PEER_EXPERTS = PEER_NKEYS * PEER_NKEYS
PEER_HALF = 128
PEER_TOPK = 16
EPS = 1e-6

LANES = 128
SUBLANES = 8
BF16_ROWS = 16
GATE_BLOCK = 4
INT_MIN = -2147483648
VMEM_LIMIT = 56 * 1024 * 1024

F32 = jnp.float32
BF16 = jnp.bfloat16

_C_QN = 0
_C_QR = _C_QN + 384
_C_QRR = _C_QR + 128
_C_CKV = _C_QRR + 128
_C_KR = _C_CKV + 128
_C_KRR = _C_KR + 128
_C_IQ = _C_KRR + 128
_C_IQR = _C_IQ + 512
_C_IK = _C_IQR + 512
_C_IKR = _C_IK + 128
_C_IW = _C_IKR + 128
_C_XB = _C_IW + 128
_C_GATE = _C_XB + 512
_C_TOTAL = _C_GATE + 512


def _cparams(sem):
    return pltpu.CompilerParams(dimension_semantics=sem, vmem_limit_bytes=VMEM_LIMIT)


def _rms(x, g):
    return x * lax.rsqrt(jnp.mean(x * x, axis=-1, keepdims=True) + EPS) * g


def _dot(a, b):
    return jnp.dot(a, b, preferred_element_type=F32)


def _dot_nt(a, b):
    return lax.dot_general(a, b, (((1,), (1,)), ((), ())), preferred_element_type=F32)


def _rot_half(w, half):
    return jnp.concatenate([-w[:, half:2 * half], w[:, :half]], axis=1)


def _pack_in_weights(w_in, w_uk):
    d = w_in.shape[0]
    o = 0
    wq = w_in[:, o:o + 512]; o += 512
    wckv = w_in[:, o:o + 128]; o += 128
    wkr = w_in[:, o:o + 32]; o += 32
    wiq = w_in[:, o:o + 512]; o += 512
    wik = w_in[:, o:o + 64]; o += 64
    wiw = w_in[:, o:o + 8]; o += 8
    wxb = w_in[:, o:o + 512]; o += 512
    wgate = w_in[:, o:o + 512]

    qn, qr, qrr = [], [], []
    for h in range(ATT_HEADS):
        wh = wq[:, h * 128:(h + 1) * 128]
        qr.append(wh[:, :ROPE_DIM])
        qrr.append(_rot_half(wh[:, :ROPE_DIM], ROPE_DIM // 2))
        qn.append(wh[:, ROPE_DIM:])
    iqr = []
    zpad = jnp.zeros((d, IDX_DIM - IDX_ROPE_DIM), w_in.dtype)
    for h in range(IDX_HEADS):
        wh = wiq[:, h * 64:(h + 1) * 64]
        iqr.append(jnp.concatenate([_rot_half(wh[:, :IDX_ROPE_DIM], IDX_ROPE_DIM // 2), zpad], axis=1))
    ikr = jnp.concatenate([_rot_half(wik[:, :IDX_ROPE_DIM], IDX_ROPE_DIM // 2), zpad], axis=1)
    cols = [
        jnp.concatenate(qn, axis=1),
        jnp.concatenate(qr, axis=1),
        jnp.concatenate(qrr, axis=1),
        wckv,
        jnp.tile(wkr, (1, 4)),
        jnp.tile(_rot_half(wkr, ROPE_DIM // 2), (1, 4)),
        wiq,
        jnp.concatenate(iqr, axis=1),
        jnp.tile(wik, (1, 2)),
        jnp.tile(ikr, (1, 2)),
        jnp.concatenate([wiw, jnp.zeros((d, 120), w_in.dtype)], axis=1),
        wxb,
        wgate,
    ]
    w_all = jnp.concatenate(cols, axis=1).astype(BF16)
    assert w_all.shape[1] == _C_TOTAL
    wuk_bd = jnp.zeros((ATT_HEADS * NOPE_DIM, ATT_HEADS * KV_LATENT), F32)
    for h in range(ATT_HEADS):
        wuk_bd = wuk_bd.at[h * NOPE_DIM:(h + 1) * NOPE_DIM, h * KV_LATENT:(h + 1) * KV_LATENT].set(w_uk[h].T)
    return w_all, wuk_bd.astype(BF16)


def _rope_freq_rows():
    inv_a = ROPE_THETA ** (-jnp.arange(0, ROPE_DIM, 2, dtype=F32) / ROPE_DIM)
    inv_i = ROPE_THETA ** (-jnp.arange(0, IDX_ROPE_DIM, 2, dtype=F32) / IDX_ROPE_DIM)
    row_a = jnp.tile(inv_a, LANES // inv_a.shape[0])[None, :]
    head_i = jnp.concatenate([inv_i, inv_i, jnp.zeros((IDX_DIM - IDX_ROPE_DIM,), F32)])
    row_i = jnp.tile(head_i, LANES // IDX_DIM)[None, :]
    return row_a, row_i


def _inproj_kernel(x_ref, pos_ref, g_ref, fa_ref, fi_ref, w_ref, wuk_ref, gkv_ref,
                   qcat_ref, kvcat_ref, iq_ref, ik_ref, iw_ref, xb_ref, gate_ref):
    hb = _rms(x_ref[...], g_ref[...]).astype(BF16)
    pos = pos_ref[...]

    def proj(c0, width):
        return _dot(hb, w_ref[:, c0:c0 + width])

    ang_a = pos * fa_ref[...]
    cos_a, sin_a = jnp.cos(ang_a), jnp.sin(ang_a)
    ang_i = pos * fi_ref[...]
    cos_i, sin_i = jnp.cos(ang_i), jnp.sin(ang_i)

    q_lat = _dot(proj(_C_QN, 384).astype(BF16), wuk_ref[...])
    q_rope = proj(_C_QR, 128) * cos_a + proj(_C_QRR, 128) * sin_a
    lane = lax.broadcasted_iota(jnp.int32, q_rope.shape, 1)
    for h in range(ATT_HEADS):
        qcat_ref[h, :, 0:128] = q_lat[:, h * 128:(h + 1) * 128].astype(BF16)
        keep = (lane >> 5) == h
        qcat_ref[h, :, 128:256] = jnp.where(keep, q_rope, 0.0).astype(BF16)

    kvcat_ref[:, 0:128] = _rms(proj(_C_CKV, 128), gkv_ref[...]).astype(BF16)
    kvcat_ref[:, 128:256] = (proj(_C_KR, 128) * cos_a + proj(_C_KRR, 128) * sin_a).astype(BF16)

    for c in range(4):
        sl = slice(c * 128, (c + 1) * 128)
        iq_ref[:, sl] = (proj(_C_IQ + c * 128, 128) * cos_i + proj(_C_IQR + c * 128, 128) * sin_i).astype(BF16)
    ik_ref[...] = (proj(_C_IK, 128) * cos_i + proj(_C_IKR, 128) * sin_i).astype(BF16)
    iw_ref[...] = proj(_C_IW, 128) * (IDX_HEADS ** -0.5 * IDX_DIM ** -0.5)
    xb_ref[...] = proj(_C_XB, 512)
    gate_ref[...] = proj(_C_GATE, 512)


def _in_projection(x2, posf, g_mix, w_all, wuk_bd, g_kv, tm):
    n = x2.shape[0]
    row_a, row_i = _rope_freq_rows()
    tok = lambda w: pl.BlockSpec((tm, w), lambda i: (i, 0))
    full = lambda a: pl.BlockSpec(a.shape, lambda i: (0,) * a.ndim)
    outs = [
        jax.ShapeDtypeStruct((ATT_HEADS, n, 256), BF16),
        jax.ShapeDtypeStruct((n, 256), BF16),
        jax.ShapeDtypeStruct((n, 512), BF16),
        jax.ShapeDtypeStruct((n, 128), BF16),
        jax.ShapeDtypeStruct((n, 128), F32),
        jax.ShapeDtypeStruct((n, 512), F32),
        jax.ShapeDtypeStruct((n, 512), F32),
    ]
    g2 = g_mix[None, :]
    gkv2 = g_kv[None, :]
    return pl.pallas_call(
        _inproj_kernel,
        grid=(n // tm,),
        in_specs=[tok(D_MODEL), tok(1), full(g2), full(row_a), full(row_i), full(w_all), full(wuk_bd), full(gkv2)],
        out_specs=[pl.BlockSpec((ATT_HEADS, tm, 256), lambda i: (0, i, 0))] + [tok(s.shape[1]) for s in outs[1:]],
        out_shape=outs,
        compiler_params=_cparams(("parallel",)),
        name="in_projection",
    )(x2, posf, g2, row_a, row_i, w_all, wuk_bd, gkv2)


def _ordered_bits_to_float(u):
    s = u ^ INT_MIN
    return pltpu.bitcast(s ^ ((s >> 31) & jnp.int32(0x7FFFFFFF)), F32)


def _tree(op, xs):
    while len(xs) > 1:
        xs = [op(xs[i], xs[i + 1]) if i + 1 < len(xs) else xs[i] for i in range(0, len(xs), 2)]
    return xs[0]


def _col_sum8(x):
    return jnp.sum(x.reshape(x.shape[0] // SUBLANES, SUBLANES, x.shape[1]), axis=0)


def _dsa_kernel(iq_ref, iw_ref, qcat_ref, ik_ref, kvcat_ref, kvt_ref, wuvt_ref, out_ref,
                iqm_s, score_s, lg_s, o_s, fv_s, fp_s, *, seq, tq, kc, topk, nb):
    j = pl.program_id(1)
    t0 = j * tq
    nk = (t0 + tq + kc - 1) // kc
    tpos = t0 + lax.broadcasted_iota(jnp.int32, (kc, tq), 1)
    row = lax.broadcasted_iota(jnp.int32, (kc, tq), 0)
    batch = range(nb)

    lane = lax.broadcasted_iota(jnp.int32, (tq, LANES), 1)
    for g in batch:
        for h in range(IDX_HEADS):
            pair = iq_ref[g, :, (h // 2) * 128:(h // 2 + 1) * 128]
            keep = (lane < IDX_DIM) if h % 2 == 0 else (lane >= IDX_DIM)
            iqm_s[g, h * tq:(h + 1) * tq, :] = jnp.where(keep, pair, jnp.zeros_like(pair))

    def score_chunk(c, carry):
        k0 = pl.multiple_of(c * kc, kc)
        for g in batch:
            r = jnp.maximum(_dot_nt(ik_ref[g, pl.ds(k0, kc), :], iqm_s[g]), 0.0) * iw_ref[g, 0]
            sc = r[:, 0:tq]
            for h in range(1, IDX_HEADS):
                sc = sc + r[:, h * tq:(h + 1) * tq]
            score_s[g, pl.ds(k0, kc), :] = jnp.where(k0 + row <= tpos, sc, -jnp.inf)
        return carry

    lax.fori_loop(0, nk, score_chunk, 0)

    def fold(fn, op, reduce_rows, init):
        def body(c, accs):
            k0 = pl.multiple_of(c * kc, kc)
            out = []
            for g in batch:
                v = fn(g, score_s[g, pl.ds(k0, kc), :], k0)
                out.append(op(accs[g], _tree(op, [v[i * SUBLANES:(i + 1) * SUBLANES] for i in range(kc // SUBLANES)])))
            return tuple(out)
        init_acc = jnp.full((SUBLANES, tq), init, jnp.asarray(init).dtype)
        accs = lax.fori_loop(0, nk, body, tuple(init_acc for _ in batch))
        return [reduce_rows(a, axis=0, keepdims=True) for a in accs]

    def count(pred):
        return fold(lambda g, blk, k0: pred(g, blk, k0).astype(jnp.int32), jnp.add, jnp.sum, jnp.int32(0))

    def bit_pass(count_ge, i, t_us):
        cand_us = [t_u | jnp.left_shift(jnp.int32(1), 31 - i) for t_u in t_us]
        cnts = count_ge([_ordered_bits_to_float(c) for c in cand_us])
        return tuple(jnp.where(cnts[g] >= topk, cand_us[g], t_us[g]) for g in batch)

    def search(n_chunks, _):
        def count_ge(cands):
            accs = [jnp.zeros((SUBLANES, tq), jnp.int32) for _ in batch]
            for c in range(n_chunks):
                for g in batch:
                    hit = (score_s[g, c * kc:(c + 1) * kc, :] >= cands[g]).astype(jnp.int32)
                    accs[g] = accs[g] + _tree(jnp.add, [hit[i * SUBLANES:(i + 1) * SUBLANES] for i in range(kc // SUBLANES)])
            return [jnp.sum(a, axis=0, keepdims=True) for a in accs]

        return lax.fori_loop(0, 32, functools.partial(bit_pass, count_ge),
                             tuple(jnp.zeros((1, tq), jnp.int32) for _ in batch))

    t_us = lax.switch(nk - 1, [functools.partial(search, n) for n in range(1, seq // kc + 1)], 0)
    thr = [_ordered_bits_to_float(jnp.where((t_u >> 23) == 0, jnp.int32(0x00800000), t_u)) for t_u in t_us]

    n_ge = count(lambda g, blk, k0: blk >= thr[g])
    for g in batch:
        fv_s[g] = thr[g]
        fp_s[g] = jnp.full((1, tq), seq, jnp.int32)
    extra = _tree(jnp.maximum, [jnp.max(n) for n in n_ge]) - topk

    @pl.when(extra > 0)
    def _():
        def drop_one(i, carry):
            fv, fp, kept = carry

            def kept_scores(g, blk, k0):
                pos = k0 + row
                return jnp.where(blk == fv[g], jnp.where(pos < fp[g], blk, jnp.inf), jnp.where(blk > fv[g], blk, jnp.inf))

            low = fold(kept_scores, jnp.minimum, jnp.min, jnp.float32(jnp.inf))

            def kept_pos_at_low(g, blk, k0):
                pos = k0 + row
                kept_pos = jnp.where(blk == fv[g], jnp.where(pos < fp[g], pos, -1), jnp.where(blk > fv[g], pos, -1))
                return jnp.where(blk == low[g], kept_pos, -1)

            last = fold(kept_pos_at_low, jnp.maximum, jnp.max, jnp.int32(-1))
            over = [kept[g] > topk for g in batch]
            return (tuple(jnp.where(over[g], low[g], fv[g]) for g in batch),
                    tuple(jnp.where(over[g], last[g], fp[g]) for g in batch),
                    tuple(jnp.where(over[g], kept[g] - 1, kept[g]) for g in batch))

        start = (tuple(thr), tuple(jnp.full((1, tq), seq, jnp.int32) for _ in batch), tuple(n_ge))
        fv, fp, _ = lax.fori_loop(0, extra, drop_one, start)
        for g in batch:
            fv_s[g] = fv[g]
            fp_s[g] = fp[g]

    floor_v = [fv_s[g] for g in batch]
    floor_p = [fp_s[g] for g in batch]
    qcat = [qcat_ref[:, g].reshape(ATT_HEADS * tq, 256) for g in batch]
    scale = ATT_HEAD_DIM ** -0.5

    def logit_chunk(c, m8s):
        k0 = pl.multiple_of(c * kc, kc)
        out = []
        for g in batch:
            blk = score_s[g, pl.ds(k0, kc), :]
            sel = jnp.where(blk == floor_v[g], (k0 + row < floor_p[g]).astype(jnp.int32),
                            (blk > floor_v[g]).astype(jnp.int32))
            bias = jnp.where(sel > 0, 0.0, -jnp.inf)
            lg = _dot_nt(kvcat_ref[g, pl.ds(k0, kc), :], qcat[g]) * scale
            lg = lg + jnp.concatenate([bias] * ATT_HEADS, axis=1)
            lg_s[g, pl.ds(k0, kc), :] = lg
            out.append(jnp.maximum(m8s[g], jnp.max(lg.reshape(kc // SUBLANES, SUBLANES, ATT_HEADS * tq), axis=0)))
        return tuple(out)

    m8s = lax.fori_loop(0, nk, logit_chunk,
                        tuple(jnp.full((SUBLANES, ATT_HEADS * tq), -jnp.inf, F32) for _ in batch))
    m = [jnp.max(m8, axis=0, keepdims=True) for m8 in m8s]
    o_s[...] = jnp.zeros_like(o_s)

    def pv_chunk(c, den8s):
        k0 = pl.multiple_of(c * kc, kc)
        out = []
        for g in batch:
            p = jnp.exp(lg_s[g, pl.ds(k0, kc), :] - m[g])
            o_s[g] += _dot(kvt_ref[g, c], p.astype(BF16))
            out.append(den8s[g] + _col_sum8(p))
        return tuple(out)

    den8s = lax.fori_loop(0, nk, pv_chunk, tuple(jnp.zeros((SUBLANES, ATT_HEADS * tq), F32) for _ in batch))
    for g in batch:
        o = (o_s[g] / jnp.sum(den8s[g], axis=0, keepdims=True)).astype(BF16)
        outs = [_dot(wuvt_ref[h], o[:, h * tq:(h + 1) * tq]) for h in range(ATT_HEADS)]
        out_ref[g] = jnp.concatenate(outs, axis=0).T.astype(BF16)


def _dsa_attention(iq, iw_rows, qcat, ik2, kvcat, kvt, wuvt, tq, kc, nb):
    b, s, _ = iq.shape
    topk = min(IDX_TOPK_MAX, s // 4)
    assert s & (s - 1) == 0, "sequence length must be a power of two"
    assert kc >= topk and s % kc == 0 and s % tq == 0 and b % nb == 0
    qb = lambda w: pl.BlockSpec((nb, tq, w), lambda i, j: (i, j, 0))
    kb = lambda w: pl.BlockSpec((nb, s, w), lambda i, j: (i, 0, 0))
    per_batch = lambda shape, dt: pltpu.VMEM((nb,) + shape, dt)
    return pl.pallas_call(
        functools.partial(_dsa_kernel, seq=s, tq=tq, kc=kc, topk=topk, nb=nb),
        grid=(b // nb, s // tq),
        in_specs=[qb(512),
                  pl.BlockSpec((nb, 1, 1, IDX_HEADS * tq), lambda i, j: (i, j, 0, 0)),
                  pl.BlockSpec((ATT_HEADS, nb, tq, 256), lambda i, j: (0, i, j, 0)),
                  kb(128), kb(256),
                  pl.BlockSpec((nb, s // kc, KV_LATENT, kc), lambda i, j: (i, 0, 0, 0)),
                  pl.BlockSpec(wuvt.shape, lambda i, j: (0, 0, 0))],
        out_specs=qb(512),
        out_shape=jax.ShapeDtypeStruct((b, s, 512), BF16),
        scratch_shapes=[per_batch((IDX_HEADS * tq, LANES), BF16),
                        per_batch((s, tq), F32),
                        per_batch((s, ATT_HEADS * tq), F32),
                        per_batch((KV_LATENT, ATT_HEADS * tq), F32),
                        per_batch((1, tq), F32), per_batch((1, tq), jnp.int32)],
        compiler_params=_cparams(("parallel", "parallel")),
        name="dsa_attention",
    )(iq, iw_rows, qcat, ik2, kvcat, kvt, wuvt)


def _lru_kernel(xb_ref, gate_ref, cw_ref, cb_ref, wrg_ref, brg_ref, wig_ref, big_ref, lam_ref, out_ref,
                xpad_s, h_s, *, ts):
    @pl.when(pl.program_id(1) == 0)
    def _():
        xpad_s[0:8, :] = jnp.zeros((8, LRU_WIDTH), F32)
        h_s[...] = jnp.zeros_like(h_s)

    xb = xb_ref[0]
    xpad_s[8:8 + ts, :] = xb
    xc = cb_ref[...]
    for w in range(CONV_WIDTH):
        off = 8 - (CONV_WIDTH - 1) + w
        xc = xc + cw_ref[w:w + 1, :] * xpad_s[off:off + ts, :]
    xpad_s[0:8, :] = xb[ts - 8:ts, :]

    xcb = xc.astype(BF16)
    r = jax.nn.sigmoid(_dot(xcb, wrg_ref[...]) + brg_ref[...])
    i = jax.nn.sigmoid(_dot(xcb, wig_ref[...]) + big_ref[...])
    log_a = -LRU_C * r * jax.nn.softplus(-lam_ref[...])
    a = jnp.exp(log_a)
    bv = jnp.sqrt(1.0 - jnp.exp(2.0 * log_a)) * (i * xc)

    row = lax.broadcasted_iota(jnp.int32, (ts, LRU_WIDTH), 0)
    d = 1
    while d < ts:
        a_sh = jnp.where(row >= d, pltpu.roll(a, d, 0), 1.0)
        b_sh = jnp.where(row >= d, pltpu.roll(bv, d, 0), 0.0)
        bv = a * b_sh + bv
        a = a * a_sh
        d *= 2
    h = a * h_s[...] + bv
    h_s[...] = h[ts - 1:ts, :]
    out_ref[0] = (h * jax.nn.gelu(gate_ref[0])).astype(BF16)


def _block_diag(w):
    nb, bi, bo = w.shape
    out = jnp.zeros((nb * bi, nb * bo), w.dtype)
    for k in range(nb):
        out = out.at[k * bi:(k + 1) * bi, k * bo:(k + 1) * bo].set(w[k])
    return out


def _rg_lru(xb, gate, conv_w, conv_b, w_rg, b_rg, w_ig, b_ig, lam, ts):
    b, s, c = xb.shape
    row = lambda a: a[None, :]
    args = [conv_w, row(conv_b), _block_diag(w_rg).astype(BF16), row(b_rg),
            _block_diag(w_ig).astype(BF16), row(b_ig), row(lam)]
    tb = pl.BlockSpec((1, ts, c), lambda i, j: (i, j, 0))
    full = lambda a: pl.BlockSpec(a.shape, lambda i, j: (0,) * a.ndim)
    return pl.pallas_call(
        functools.partial(_lru_kernel, ts=ts),
        grid=(b, s // ts),
        in_specs=[tb, tb] + [full(a) for a in args],
        out_specs=tb,
        out_shape=jax.ShapeDtypeStruct((b, s, c), BF16),
        scratch_shapes=[pltpu.VMEM((ts + 8, c), F32), pltpu.VMEM((1, c), F32)],
        compiler_params=_cparams(("parallel", "arbitrary")),
        name="rg_lru",
    )(xb, gate, *args)


def _outproj_kernel(x_ref, att_ref, rec_ref, wo_ref, g_ref, wpq_ref, k1_ref, k2_ref,
                    x1_ref, h2t_ref, s1_ref, s2_ref):
    x1 = x_ref[...] + _dot(att_ref[...], wo_ref[0:512, :]) + _dot(rec_ref[...], wo_ref[512:1024, :])
    x1_ref[...] = x1
    h2f = _rms(x1, g_ref[...])
    h2t_ref[...] = h2f.T.astype(BF16)
    h2 = h2f.astype(BF16)
    for h in range(PEER_HEADS):
        q = _dot(h2, wpq_ref[:, h * 256:(h + 1) * 256]).astype(BF16)
        s1_ref[h] = _dot_nt(k1_ref[h], q[:, :PEER_HALF])
        s2_ref[h] = _dot_nt(k2_ref[h], q[:, PEER_HALF:])


def _out_projection(x2, att, rec, w_out, g_ffn, w_pq, k1, k2, tm):
    n = x2.shape[0]
    g2 = g_ffn[None, :]
    tok = lambda w: pl.BlockSpec((tm, w), lambda i: (i, 0))
    full = lambda a: pl.BlockSpec(a.shape, lambda i: (0,) * a.ndim)
    sblk = pl.BlockSpec((PEER_HEADS, PEER_NKEYS, tm), lambda i: (0, 0, i))
    s_shape = jax.ShapeDtypeStruct((PEER_HEADS, PEER_NKEYS, n), F32)
    return pl.pallas_call(
        _outproj_kernel,
        grid=(n // tm,),
        in_specs=[tok(D_MODEL), tok(512), tok(512), full(w_out), full(g2), full(w_pq), full(k1), full(k2)],
        out_specs=[tok(D_MODEL), pl.BlockSpec((D_MODEL, tm), lambda i: (0, i)), sblk, sblk],
        out_shape=[jax.ShapeDtypeStruct((n, D_MODEL), F32), jax.ShapeDtypeStruct((D_MODEL, n), BF16),
                   s_shape, s_shape],
        compiler_params=_cparams(("parallel",)),
        name="out_projection",
    )(x2, att, rec, w_out, g2, w_pq, k1, k2)


def _top_values(s, k):
    vals = []
    for _ in range(k):
        m = jnp.max(s, axis=0, keepdims=True)
        vals.append(m)
        s = jnp.where(s == m, -jnp.inf, s)
    return vals


def _kth_largest(s, k):
    kth = None
    above = jnp.zeros((1, s.shape[1]), F32)
    for _ in range(k):
        m = jnp.max(s, axis=0, keepdims=True)
        hit = s == m
        kth = m if kth is None else jnp.where(above < k, m, kth)
        above = above + jnp.sum(jnp.where(hit, 1.0, 0.0), axis=0, keepdims=True)
        s = jnp.where(hit, -jnp.inf, s)
    return kth


def _peer_select_kernel(s1_ref, s2_ref, cnt_ref, p1_ref, rank_ref, p2_ref, v1_s, v2_s):
    s1 = s1_ref[0]
    s2 = s2_ref[0]
    t = s1.shape[1]
    v1 = _top_values(s1, PEER_TOPK)
    for i in range(PEER_TOPK):
        v1_s[i:i + 1, :] = v1[i]
    v1_all = v1_s[...]
    v2 = []
    cur = s2
    rank = jnp.full(s2.shape, float(PEER_TOPK), F32)
    for j in range(PEER_TOPK):
        m = jnp.max(cur, axis=0, keepdims=True)
        v2.append(m)
        hit = cur == m
        rank = jnp.where(hit, float(j), rank)
        cur = jnp.where(hit, -jnp.inf, cur)
        v2_s[j:j + 1, :] = m
    v2_all = v2_s[...]

    m1, m2 = v1[0], v2[0]
    e1_all = jnp.exp(v1_all - m1)
    e2_all = jnp.exp(v2_all - m2)
    row8 = lax.broadcasted_iota(jnp.int32, (SUBLANES, t), 0)
    slabs = [(v1[0] + v2_all[0:8], e1_all[0:1] * e2_all[0:8]),
             (v1[0] + v2_all[8:16], e1_all[0:1] * e2_all[8:16])]
    for i in range(1, 8):
        keep = row8 < PEER_TOPK // (i + 1)
        slabs.append((jnp.where(keep, v1[i] + v2_all[0:8], -jnp.inf), e1_all[i:i + 1] * e2_all[0:8]))
    slabs.append((v1_all[8:16] + v2[0], e1_all[8:16] * e2_all[0:1]))
    cand = jnp.concatenate([c for c, _ in slabs], axis=0)
    thr = _kth_largest(cand, PEER_TOPK)
    z = jnp.sum(jnp.where(cand >= thr, jnp.concatenate([e for _, e in slabs], axis=0), 0.0), axis=0, keepdims=True)

    cnt_top = jnp.zeros((PEER_TOPK, t), F32)
    for j in range(PEER_TOPK):
        cnt_top = cnt_top + jnp.where(v1_all + v2[j] >= thr, 1.0, 0.0)
    cnt = jnp.zeros(s1.shape, F32)
    for i in range(PEER_TOPK):
        cnt = jnp.where(s1 == v1[i], cnt_top[i:i + 1], cnt)
    cnt_ref[0] = cnt
    p1_ref[0] = jnp.exp(s1 - m1) / z
    p2 = jnp.exp(s2 - m2)
    for c in range(t // LANES):
        rank_ref[0, c] = rank[:, c * LANES:(c + 1) * LANES]
        p2_ref[0, c] = p2[:, c * LANES:(c + 1) * LANES]


def _peer_select(s1t, s2t, tb):
    n = s1t.shape[2]
    blk = pl.BlockSpec((1, PEER_NKEYS, tb), lambda h, i: (h, 0, i))
    f32s = jax.ShapeDtypeStruct(s1t.shape, F32)
    tiles = jax.ShapeDtypeStruct((PEER_HEADS, n // LANES, PEER_NKEYS, LANES), F32)
    tile_blk = pl.BlockSpec((1, tb // LANES, PEER_NKEYS, LANES), lambda h, i: (h, i, 0, 0))
    return pl.pallas_call(
        _peer_select_kernel,
        grid=(PEER_HEADS, n // tb),
        in_specs=[blk, blk],
        out_specs=[blk, blk, tile_blk, tile_blk],
        out_shape=[f32s, f32s, tiles, tiles],
        scratch_shapes=[pltpu.VMEM((PEER_TOPK, tb), F32), pltpu.VMEM((PEER_TOPK, tb), F32)],
        compiler_params=_cparams(("parallel", "parallel")),
        name="peer_select",
    )(s1t, s2t)


def _gelu_exact(x):
    return 0.5 * x * (1.0 + lax.erf(x * (2.0 ** -0.5)))


def _peer_dense_kernel(h2t_ref, u_ref, vt_ref, cnt_ref, p1_ref, rank_ref, p2_ref, out_ref,
                       acc_s, act_s, wg_s, cnt_s, p1_s, *, rows_per_step):
    e = pl.program_id(1)

    @pl.when(e == 0)
    def _():
        acc_s[...] = jnp.zeros_like(acc_s)

    act = _dot(u_ref[...], h2t_ref[...])
    n_tiles = act_s.shape[0]
    tb = n_tiles * LANES
    for tt in range(n_tiles):
        act_s[tt] = act[:, tt * LANES:(tt + 1) * LANES]
    for h in range(PEER_HEADS):
        for r in range(rows_per_step):
            i = h * rows_per_step + r
            row_c = cnt_ref[h, pl.ds(e * rows_per_step + r, 1), :]
            row_p = p1_ref[h, pl.ds(e * rows_per_step + r, 1), :]
            for tt in range(tb // LANES):
                ts = slice(tt * LANES, (tt + 1) * LANES)
                cnt_s[tt, i] = jnp.broadcast_to(row_c[:, ts], (SUBLANES, LANES))
                p1_s[tt, i] = jnp.broadcast_to(row_p[:, ts], (SUBLANES, LANES))

    def token_tile(tt, carry):
        key_tiles = PEER_NKEYS // SUBLANES
        for k0 in range(0, key_tiles, GATE_BLOCK):
            for r0 in range(0, rows_per_step, GATE_BLOCK):
                kk = range(k0, k0 + GATE_BLOCK)
                rr = range(r0, r0 + GATE_BLOCK)
                w = {(k, r): jnp.zeros((SUBLANES, LANES), F32) for k in kk for r in rr}
                for h in range(PEER_HEADS):
                    rank = {k: rank_ref[h, tt, k * SUBLANES:(k + 1) * SUBLANES, :] for k in kk}
                    p2 = {k: p2_ref[h, tt, k * SUBLANES:(k + 1) * SUBLANES, :] for k in kk}
                    for r in rr:
                        cnt = cnt_s[tt, h * rows_per_step + r]
                        p1 = p1_s[tt, h * rows_per_step + r]
                        for k in kk:
                            w[k, r] = w[k, r] + p1 * jnp.where(rank[k] < cnt, p2[k], 0.0)
                for r in rr:
                    for k in range(k0, k0 + GATE_BLOCK, 2):
                        es = slice(r * PEER_NKEYS + k * SUBLANES, r * PEER_NKEYS + (k + 2) * SUBLANES)
                        w2 = jnp.concatenate([w[k, r], w[k + 1, r]], axis=0)
                        wg_s[tt, es, :] = (w2 * _gelu_exact(act_s[tt, es, :])).astype(BF16)
        return carry

    lax.fori_loop(0, n_tiles, token_tile, 0)
    wg = jnp.concatenate([wg_s[tt] for tt in range(n_tiles)], axis=1)
    acc_s[...] += _dot(vt_ref[...], wg)

    @pl.when(e == pl.num_programs(1) - 1)
    def _():
        out_ref[...] = acc_s[...].T


def _peer_dense(h2t, u_bf, vt_bf, cnt, p1, rank, p2, tb, eb):
    n = h2t.shape[1]
    rows = eb // PEER_NKEYS
    rowblk = pl.BlockSpec((PEER_HEADS, PEER_NKEYS, tb), lambda i, e: (0, 0, i))
    tileblk = pl.BlockSpec((PEER_HEADS, tb // LANES, PEER_NKEYS, LANES), lambda i, e: (0, i, 0, 0))
    staged = pltpu.VMEM((tb // LANES, PEER_HEADS * rows, SUBLANES, LANES), F32)
    return pl.pallas_call(
        functools.partial(_peer_dense_kernel, rows_per_step=rows),
        grid=(n // tb, PEER_EXPERTS // eb),
        in_specs=[pl.BlockSpec((D_MODEL, tb), lambda i, e: (0, i)),
                  pl.BlockSpec((eb, D_MODEL), lambda i, e: (e, 0)),
                  pl.BlockSpec((D_MODEL, eb), lambda i, e: (0, e)),
                  rowblk, rowblk, tileblk, tileblk],
        out_specs=pl.BlockSpec((tb, D_MODEL), lambda i, e: (i, 0)),
        out_shape=jax.ShapeDtypeStruct((n, D_MODEL), F32),
        scratch_shapes=[pltpu.VMEM((D_MODEL, tb), F32), pltpu.VMEM((tb // LANES, eb, LANES), F32),
                        pltpu.VMEM((tb // LANES, eb, LANES), BF16), staged, staged],
        compiler_params=_cparams(("parallel", "arbitrary")),
        name="peer_dense",
    )(h2t, u_bf, vt_bf, cnt, p1, rank, p2)


def _final_kernel(x1_ref, peer_ref, p_ref, wg_ref, wp_ref, g_ref, out_ref, *, normalize):
    x2 = x1_ref[...] + peer_ref[...]
    gate = jax.nn.sigmoid(_dot(x2.astype(BF16), wg_ref[...]))
    x3 = x2 + gate * _dot(p_ref[...].astype(BF16), wp_ref[...])
    out_ref[...] = _rms(x3, g_ref[...]) if normalize else x3


def _final(x1, peer, p2, w_gate, w_ple, g_final, tm, normalize):
    n = x1.shape[0]
    g2 = g_final[None, :]
    tok = lambda w: pl.BlockSpec((tm, w), lambda i: (i, 0))
    full = lambda a: pl.BlockSpec(a.shape, lambda i: (0,) * a.ndim)
    return pl.pallas_call(
        functools.partial(_final_kernel, normalize=normalize),
        grid=(n // tm,),
        in_specs=[tok(D_MODEL), tok(D_MODEL), tok(PLE_DIM), full(w_gate), full(w_ple), full(g2)],
        out_specs=tok(D_MODEL),
        out_shape=jax.ShapeDtypeStruct((n, D_MODEL), F32),
        compiler_params=_cparams(("parallel",)),
        name="ple_final_norm",
    )(x1, peer, p2, w_gate, w_ple, g2)


def _layer(x2, p2, posf, b, s, normalize, g_mix, w_in, g_kv, w_uk, w_uv, conv_w, conv_b, w_rg, b_rg, w_ig, b_ig,
           lam, w_out, g_ffn, w_pq, k1, k2, u_tab, v_tab, w_ple, w_ple_gate, g_final):
    n = b * s
    tm = min(512, n)
    tq = 128
    kc = 256
    w_all, wuk_bd = _pack_in_weights(w_in, w_uk)
    qcat, kvcat, iq, ik2, iw, xb, gate = _in_projection(x2, posf, g_mix, w_all, wuk_bd, g_kv, tm)

    r3 = lambda a: a.reshape(b, s, a.shape[-1])
    iw_rows = iw[:, :IDX_HEADS].reshape(b, s // tq, tq, IDX_HEADS).swapaxes(2, 3).reshape(b, s // tq, 1, IDX_HEADS * tq)
    kvt = r3(kvcat)[:, :, :KV_LATENT].reshape(b, s // kc, kc, KV_LATENT).swapaxes(2, 3)
    wuvt = jnp.swapaxes(w_uv, 1, 2).astype(BF16)
    att = _dsa_attention(r3(iq), iw_rows, qcat.reshape(ATT_HEADS, b, s, 256), r3(ik2), r3(kvcat), kvt, wuvt, tq, kc, nb=4 if b % 4 == 0 else 1)
    rec = _rg_lru(r3(xb), r3(gate), conv_w, conv_b, w_rg, b_rg, w_ig, b_ig, lam, ts=min(256, s))

    x1, h2t, s1t, s2t = _out_projection(x2, att.reshape(n, 512), rec.reshape(n, 512), w_out.astype(BF16), g_ffn,
                                        w_pq.astype(BF16), k1.astype(BF16), k2.astype(BF16), tm)
    cnt, p1, rank, p2f = _peer_select(s1t, s2t, tb=min(512, n))
    peer = _peer_dense(h2t, u_tab.astype(BF16), v_tab.T.astype(BF16), cnt, p1, rank, p2f,
                       tb=min(512, n), eb=2048)
    return _final(x1, peer, p2, w_ple_gate.astype(BF16), w_ple.astype(BF16), g_final, tm, normalize)


def kernel(x, p, positions, g_mix, w_in, g_kv, w_uk, w_uv, conv_w, conv_b, w_rg, b_rg, w_ig, b_ig, lru_lambda,
           w_out, g_ffn, w_pq, peer_k1, peer_k2, peer_u, peer_v, w_ple, w_ple_gate, g_final):
    b, s, d = x.shape
    n = b * s
    depth = w_in.shape[0]
    posf = positions.astype(F32).reshape(n, 1)
    x2 = x.reshape(n, d)
    for i in range(depth):
        x2 = _layer(x2, p[i].reshape(n, PLE_DIM), posf, b, s, i == depth - 1, g_mix[i], w_in[i], g_kv[i], w_uk[i],
                    w_uv[i], conv_w[i], conv_b[i], w_rg[i], b_rg[i], w_ig[i], b_ig[i], lru_lambda[i], w_out[i],
                    g_ffn[i], w_pq[i], peer_k1[i], peer_k2[i], peer_u[i], peer_v[i], w_ple[i], w_ple_gate[i],
                    g_final)
    return x2.reshape(b, s, d)
```

```python
import functools

import jax
import jax.numpy as jnp
from jax import lax
from jax.experimental import pallas as pl
from jax.experimental.pallas import tpu as pltpu

D_MODEL = 1024
ATT_HEADS = 4
ATT_HEAD_DIM = 128
ROPE_DIM = 32
NOPE_DIM = ATT_HEAD_DIM - ROPE_DIM
KV_LATENT = 128
IDX_HEADS = 8
IDX_DIM = 64
IDX_ROPE_DIM = 16
IDX_TOPK_MAX = 256
LRU_WIDTH = 512
CONV_WIDTH = 4
LRU_C = 8.0
ROPE_THETA = 500000.0
PLE_DIM = 256
PEER_HEADS = 8
PEER_NKEYS = 128
PEER_EXPERTS = PEER_NKEYS * PEER_NKEYS
PEER_HALF = 128
PEER_TOPK = 16
EPS = 1e-6

LANES = 128
SUBLANES = 8
GATE_BLOCK = 4
INT_MIN = -2147483648
VMEM_LIMIT = 56 * 1024 * 1024

F32 = jnp.float32
BF16 = jnp.bfloat16

_C_QN = 0
_C_QR = _C_QN + 384
_C_QRR = _C_QR + 128
_C_CKV = _C_QRR + 128
_C_KR = _C_CKV + 128
_C_KRR = _C_KR + 128
_C_IQ = _C_KRR + 128
_C_IQR = _C_IQ + 512
_C_IK = _C_IQR + 512
_C_IKR = _C_IK + 128
_C_IW = _C_IKR + 128
_C_XB = _C_IW + 128
_C_GATE = _C_XB + 512
_C_TOTAL = _C_GATE + 512


def _cparams(sem):
    return pltpu.CompilerParams(dimension_semantics=sem, vmem_limit_bytes=VMEM_LIMIT)


def _rms(x, g):
    return x * lax.rsqrt(jnp.mean(x * x, axis=-1, keepdims=True) + EPS) * g


def _dot(a, b):
    return jnp.dot(a, b, preferred_element_type=F32)


def _dot_nt(a, b):
    return lax.dot_general(a, b, (((1,), (1,)), ((), ())), preferred_element_type=F32)


def _rot_half(w, half):
    return jnp.concatenate([-w[:, half:2 * half], w[:, :half]], axis=1)


def _pack_in_weights(w_in, w_uk):
    d = w_in.shape[0]
    o = 0
    wq = w_in[:, o:o + 512]; o += 512
    wckv = w_in[:, o:o + 128]; o += 128
    wkr = w_in[:, o:o + 32]; o += 32
    wiq = w_in[:, o:o + 512]; o += 512
    wik = w_in[:, o:o + 64]; o += 64
    wiw = w_in[:, o:o + 8]; o += 8
    wxb = w_in[:, o:o + 512]; o += 512
    wgate = w_in[:, o:o + 512]

    qn, qr, qrr = [], [], []
    for h in range(ATT_HEADS):
        wh = wq[:, h * 128:(h + 1) * 128]
        qr.append(wh[:, :ROPE_DIM])
        qrr.append(_rot_half(wh[:, :ROPE_DIM], ROPE_DIM // 2))
        qn.append(wh[:, ROPE_DIM:])
    iqr = []
    zpad = jnp.zeros((d, IDX_DIM - IDX_ROPE_DIM), w_in.dtype)
    for h in range(IDX_HEADS):
        wh = wiq[:, h * 64:(h + 1) * 64]
        iqr.append(jnp.concatenate([_rot_half(wh[:, :IDX_ROPE_DIM], IDX_ROPE_DIM // 2), zpad], axis=1))
    ikr = jnp.concatenate([_rot_half(wik[:, :IDX_ROPE_DIM], IDX_ROPE_DIM // 2), zpad], axis=1)
    cols = [
        jnp.concatenate(qn, axis=1),
        jnp.concatenate(qr, axis=1),
        jnp.concatenate(qrr, axis=1),
        wckv,
        jnp.tile(wkr, (1, 4)),
        jnp.tile(_rot_half(wkr, ROPE_DIM // 2), (1, 4)),
        wiq,
        jnp.concatenate(iqr, axis=1),
        jnp.tile(wik, (1, 2)),
        jnp.tile(ikr, (1, 2)),
        jnp.concatenate([wiw, jnp.zeros((d, 120), w_in.dtype)], axis=1),
        wxb,
        wgate,
    ]
    w_all = jnp.concatenate(cols, axis=1).astype(BF16)
    assert w_all.shape[1] == _C_TOTAL
    wuk_bd = jnp.zeros((ATT_HEADS * NOPE_DIM, ATT_HEADS * KV_LATENT), F32)
    for h in range(ATT_HEADS):
        wuk_bd = wuk_bd.at[h * NOPE_DIM:(h + 1) * NOPE_DIM, h * KV_LATENT:(h + 1) * KV_LATENT].set(w_uk[h].T)
    return w_all, wuk_bd.astype(BF16)


def _rope_freq_rows():
    inv_a = ROPE_THETA ** (-jnp.arange(0, ROPE_DIM, 2, dtype=F32) / ROPE_DIM)
    inv_i = ROPE_THETA ** (-jnp.arange(0, IDX_ROPE_DIM, 2, dtype=F32) / IDX_ROPE_DIM)
    row_a = jnp.tile(inv_a, LANES // inv_a.shape[0])[None, :]
    head_i = jnp.concatenate([inv_i, inv_i, jnp.zeros((IDX_DIM - IDX_ROPE_DIM,), F32)])
    row_i = jnp.tile(head_i, LANES // IDX_DIM)[None, :]
    return row_a, row_i


def _inproj_kernel(x_ref, pos_ref, g_ref, fa_ref, fi_ref, w_ref, wuk_ref, gkv_ref,
                   qcat_ref, kvcat_ref, iq_ref, ik_ref, iw_ref, xb_ref, gate_ref):
    hb = _rms(x_ref[...], g_ref[...]).astype(BF16)
    pos = pos_ref[...]

    y = _dot(hb, w_ref[...])

    def proj(c0, width):
        return y[:, c0:c0 + width]

    ang_a = pos * fa_ref[...]
    cos_a, sin_a = jnp.cos(ang_a), jnp.sin(ang_a)
    ang_i = pos * fi_ref[...]
    cos_i, sin_i = jnp.cos(ang_i), jnp.sin(ang_i)

    q_lat = _dot(proj(_C_QN, 384).astype(BF16), wuk_ref[...])
    q_rope = proj(_C_QR, 128) * cos_a + proj(_C_QRR, 128) * sin_a
    lane = lax.broadcasted_iota(jnp.int32, q_rope.shape, 1)
    for h in range(ATT_HEADS):
        qcat_ref[h, :, 0:128] = q_lat[:, h * 128:(h + 1) * 128].astype(BF16)
        keep = (lane >> 5) == h
        qcat_ref[h, :, 128:256] = jnp.where(keep, q_rope, 0.0).astype(BF16)

    kvcat_ref[:, 0:128] = _rms(proj(_C_CKV, 128), gkv_ref[...]).astype(BF16)
    kvcat_ref[:, 128:256] = (proj(_C_KR, 128) * cos_a + proj(_C_KRR, 128) * sin_a).astype(BF16)

    for c in range(4):
        sl = slice(c * 128, (c + 1) * 128)
        iq_ref[:, sl] = (proj(_C_IQ + c * 128, 128) * cos_i + proj(_C_IQR + c * 128, 128) * sin_i).astype(BF16)
    ik_ref[...] = (proj(_C_IK, 128) * cos_i + proj(_C_IKR, 128) * sin_i).astype(BF16)
    iw_ref[...] = proj(_C_IW, 128) * (IDX_HEADS ** -0.5 * IDX_DIM ** -0.5)
    xb_ref[...] = proj(_C_XB, 512)
    gate_ref[...] = proj(_C_GATE, 512)


def _in_projection(x2, posf, g_mix, w_all, wuk_bd, g_kv, tm):
    n = x2.shape[0]
    row_a, row_i = _rope_freq_rows()
    tok = lambda w: pl.BlockSpec((tm, w), lambda i: (i, 0))
    full = lambda a: pl.BlockSpec(a.shape, lambda i: (0,) * a.ndim)
    outs = [
        jax.ShapeDtypeStruct((ATT_HEADS, n, 256), BF16),
        jax.ShapeDtypeStruct((n, 256), BF16),
        jax.ShapeDtypeStruct((n, 512), BF16),
        jax.ShapeDtypeStruct((n, 128), BF16),
        jax.ShapeDtypeStruct((n, 128), F32),
        jax.ShapeDtypeStruct((n, 512), F32),
        jax.ShapeDtypeStruct((n, 512), F32),
    ]
    g2 = g_mix[None, :]
    gkv2 = g_kv[None, :]
    return pl.pallas_call(
        _inproj_kernel,
        grid=(n // tm,),
        in_specs=[tok(D_MODEL), tok(1), full(g2), full(row_a), full(row_i), full(w_all), full(wuk_bd), full(gkv2)],
        out_specs=[pl.BlockSpec((ATT_HEADS, tm, 256), lambda i: (0, i, 0))] + [tok(s.shape[1]) for s in outs[1:]],
        out_shape=outs,
        compiler_params=_cparams(("parallel",)),
        name="in_projection",
    )(x2, posf, g2, row_a, row_i, w_all, wuk_bd, gkv2)


def _ordered_bits_to_float(u):
    s = u ^ INT_MIN
    return pltpu.bitcast(s ^ ((s >> 31) & jnp.int32(0x7FFFFFFF)), F32)


def _tree(op, xs):
    while len(xs) > 1:
        xs = [op(xs[i], xs[i + 1]) if i + 1 < len(xs) else xs[i] for i in range(0, len(xs), 2)]
    return xs[0]


def _col_sum8(x):
    return jnp.sum(x.reshape(x.shape[0] // SUBLANES, SUBLANES, x.shape[1]), axis=0)


def _dsa_kernel(iq_ref, iw_ref, qcat_ref, ik_ref, kvcat_ref, kvt_ref, wuvt_ref, out_ref,
                iqm_s, score_s, lg_s, o_s, fv_s, fp_s, *, seq, tq, kc, topk, nb):
    j = pl.program_id(1)
    t0 = j * tq
    nk = (t0 + tq + kc - 1) // kc
    tpos = t0 + lax.broadcasted_iota(jnp.int32, (kc, tq), 1)
    row = lax.broadcasted_iota(jnp.int32, (kc, tq), 0)
    batch = range(nb)

    lane = lax.broadcasted_iota(jnp.int32, (tq, LANES), 1)
    for g in batch:
        for h in range(IDX_HEADS):
            pair = iq_ref[g, :, (h // 2) * 128:(h // 2 + 1) * 128]
            keep = (lane < IDX_DIM) if h % 2 == 0 else (lane >= IDX_DIM)
            iqm_s[g, h * tq:(h + 1) * tq, :] = jnp.where(keep, pair, jnp.zeros_like(pair))

    def score_chunk(c, carry):
        k0 = pl.multiple_of(c * kc, kc)
        for g in batch:
            r = jnp.maximum(_dot_nt(ik_ref[g, pl.ds(k0, kc), :], iqm_s[g]), 0.0) * iw_ref[g, 0]
            sc = r[:, 0:tq]
            for h in range(1, IDX_HEADS):
                sc = sc + r[:, h * tq:(h + 1) * tq]
            score_s[g, pl.ds(k0, kc), :] = jnp.where(k0 + row <= tpos, sc, -jnp.inf)
        return carry

    lax.fori_loop(0, nk, score_chunk, 0)

    def fold(fn, op, reduce_rows, init):
        def body(c, accs):
            k0 = pl.multiple_of(c * kc, kc)
            out = []
            for g in batch:
                v = fn(g, score_s[g, pl.ds(k0, kc), :], k0)
                out.append(op(accs[g], _tree(op, [v[i * SUBLANES:(i + 1) * SUBLANES] for i in range(kc // SUBLANES)])))
            return tuple(out)
        init_acc = jnp.full((SUBLANES, tq), init, jnp.asarray(init).dtype)
        accs = lax.fori_loop(0, nk, body, tuple(init_acc for _ in batch))
        return [reduce_rows(a, axis=0, keepdims=True) for a in accs]

    def count(pred):
        return fold(lambda g, blk, k0: pred(g, blk, k0).astype(jnp.int32), jnp.add, jnp.sum, jnp.int32(0))

    def bit_pass(count_ge, i, t_us):
        cand_us = [t_u | jnp.left_shift(jnp.int32(1), 31 - i) for t_u in t_us]
        cnts = count_ge([_ordered_bits_to_float(c) for c in cand_us])
        return tuple(jnp.where(cnts[g] >= topk, cand_us[g], t_us[g]) for g in batch)

    def search(n_chunks, _):
        def count_ge(cands):
            accs = [jnp.zeros((SUBLANES, tq), jnp.int32) for _ in batch]
            for c in range(n_chunks):
                for g in batch:
                    hit = (score_s[g, c * kc:(c + 1) * kc, :] >= cands[g]).astype(jnp.int32)
                    accs[g] = accs[g] + _tree(jnp.add, [hit[i * SUBLANES:(i + 1) * SUBLANES] for i in range(kc // SUBLANES)])
            return [jnp.sum(a, axis=0, keepdims=True) for a in accs]

        return lax.fori_loop(0, 32, functools.partial(bit_pass, count_ge),
                             tuple(jnp.zeros((1, tq), jnp.int32) for _ in batch))

    t_us = lax.switch(nk - 1, [functools.partial(search, n) for n in range(1, seq // kc + 1)], 0)
    thr = [_ordered_bits_to_float(jnp.where((t_u >> 23) == 0, jnp.int32(0x00800000), t_u)) for t_u in t_us]

    n_ge = count(lambda g, blk, k0: blk >= thr[g])
    for g in batch:
        fv_s[g] = thr[g]
        fp_s[g] = jnp.full((1, tq), seq, jnp.int32)
    extra = _tree(jnp.maximum, [jnp.max(n) for n in n_ge]) - topk

    @pl.when(extra > 0)
    def _():
        def drop_one(i, carry):
            fv, fp, kept = carry

            def kept_scores(g, blk, k0):
                pos = k0 + row
                return jnp.where(blk == fv[g], jnp.where(pos < fp[g], blk, jnp.inf), jnp.where(blk > fv[g], blk, jnp.inf))

            low = fold(kept_scores, jnp.minimum, jnp.min, jnp.float32(jnp.inf))

            def kept_pos_at_low(g, blk, k0):
                pos = k0 + row
                kept_pos = jnp.where(blk == fv[g], jnp.where(pos < fp[g], pos, -1), jnp.where(blk > fv[g], pos, -1))
                return jnp.where(blk == low[g], kept_pos, -1)

            last = fold(kept_pos_at_low, jnp.maximum, jnp.max, jnp.int32(-1))
            over = [kept[g] > topk for g in batch]
            return (tuple(jnp.where(over[g], low[g], fv[g]) for g in batch),
                    tuple(jnp.where(over[g], last[g], fp[g]) for g in batch),
                    tuple(jnp.where(over[g], kept[g] - 1, kept[g]) for g in batch))

        start = (tuple(thr), tuple(jnp.full((1, tq), seq, jnp.int32) for _ in batch), tuple(n_ge))
        fv, fp, _ = lax.fori_loop(0, extra, drop_one, start)
        for g in batch:
            fv_s[g] = fv[g]
            fp_s[g] = fp[g]

    floor_v = [fv_s[g] for g in batch]
    floor_p = [fp_s[g] for g in batch]
    qcat = [qcat_ref[:, g].reshape(ATT_HEADS * tq, 256) for g in batch]
    scale = ATT_HEAD_DIM ** -0.5

    def logit_chunk(c, m8s):
        k0 = pl.multiple_of(c * kc, kc)
        out = []
        for g in batch:
            blk = score_s[g, pl.ds(k0, kc), :]
            sel = jnp.where(blk == floor_v[g], (k0 + row < floor_p[g]).astype(jnp.int32),
                            (blk > floor_v[g]).astype(jnp.int32))
            bias = jnp.where(sel > 0, 0.0, -jnp.inf)
            lg = _dot_nt(kvcat_ref[g, pl.ds(k0, kc), :], qcat[g]) * scale
            lg = lg + jnp.concatenate([bias] * ATT_HEADS, axis=1)
            lg_s[g, pl.ds(k0, kc), :] = lg
            out.append(jnp.maximum(m8s[g], jnp.max(lg.reshape(kc // SUBLANES, SUBLANES, ATT_HEADS * tq), axis=0)))
        return tuple(out)

    m8s = lax.fori_loop(0, nk, logit_chunk,
                        tuple(jnp.full((SUBLANES, ATT_HEADS * tq), -jnp.inf, F32) for _ in batch))
    m = [jnp.max(m8, axis=0, keepdims=True) for m8 in m8s]
    o_s[...] = jnp.zeros_like(o_s)

    def pv_chunk(c, den8s):
        k0 = pl.multiple_of(c * kc, kc)
        out = []
        for g in batch:
            p = jnp.exp(lg_s[g, pl.ds(k0, kc), :] - m[g])
            o_s[g] += _dot(kvt_ref[g, c], p.astype(BF16))
            out.append(den8s[g] + _col_sum8(p))
        return tuple(out)

    den8s = lax.fori_loop(0, nk, pv_chunk, tuple(jnp.zeros((SUBLANES, ATT_HEADS * tq), F32) for _ in batch))
    for g in batch:
        o = (o_s[g] / jnp.sum(den8s[g], axis=0, keepdims=True)).astype(BF16)
        outs = [_dot(wuvt_ref[h], o[:, h * tq:(h + 1) * tq]) for h in range(ATT_HEADS)]
        out_ref[g] = jnp.concatenate(outs, axis=0).T.astype(BF16)


def _dsa_attention(iq, iw_rows, qcat, ik2, kvcat, kvt, wuvt, tq, kc, nb):
    b, s, _ = iq.shape
    topk = min(IDX_TOPK_MAX, s // 4)
    assert s & (s - 1) == 0, "sequence length must be a power of two"
    assert kc >= topk and s % kc == 0 and s % tq == 0 and b % nb == 0
    qb = lambda w: pl.BlockSpec((nb, tq, w), lambda i, j: (i, j, 0))
    kb = lambda w: pl.BlockSpec((nb, s, w), lambda i, j: (i, 0, 0))
    per_batch = lambda shape, dt: pltpu.VMEM((nb,) + shape, dt)
    return pl.pallas_call(
        functools.partial(_dsa_kernel, seq=s, tq=tq, kc=kc, topk=topk, nb=nb),
        grid=(b // nb, s // tq),
        in_specs=[qb(512),
                  pl.BlockSpec((nb, 1, 1, IDX_HEADS * tq), lambda i, j: (i, j, 0, 0)),
                  pl.BlockSpec((ATT_HEADS, nb, tq, 256), lambda i, j: (0, i, j, 0)),
                  kb(128), kb(256),
                  pl.BlockSpec((nb, s // kc, KV_LATENT, kc), lambda i, j: (i, 0, 0, 0)),
                  pl.BlockSpec(wuvt.shape, lambda i, j: (0, 0, 0))],
        out_specs=qb(512),
        out_shape=jax.ShapeDtypeStruct((b, s, 512), BF16),
        scratch_shapes=[per_batch((IDX_HEADS * tq, LANES), BF16),
                        per_batch((s, tq), F32),
                        per_batch((s, ATT_HEADS * tq), F32),
                        per_batch((KV_LATENT, ATT_HEADS * tq), F32),
                        per_batch((1, tq), F32), per_batch((1, tq), jnp.int32)],
        compiler_params=_cparams(("parallel", "parallel")),
        name="dsa_attention",
    )(iq, iw_rows, qcat, ik2, kvcat, kvt, wuvt)


def _lru_kernel(xb_ref, gate_ref, cw_ref, cb_ref, wrg_ref, brg_ref, wig_ref, big_ref, lam_ref, out_ref,
                xpad_s, h_s, *, ts):
    @pl.when(pl.program_id(1) == 0)
    def _():
        xpad_s[0:8, :] = jnp.zeros((8, LRU_WIDTH), F32)
        h_s[...] = jnp.zeros_like(h_s)

    xb = xb_ref[0]
    xpad_s[8:8 + ts, :] = xb
    xc = cb_ref[...]
    for w in range(CONV_WIDTH):
        off = 8 - (CONV_WIDTH - 1) + w
        xc = xc + cw_ref[w:w + 1, :] * xpad_s[off:off + ts, :]
    xpad_s[0:8, :] = xb[ts - 8:ts, :]

    xcb = xc.astype(BF16)
    r = jax.nn.sigmoid(_dot(xcb, wrg_ref[...]) + brg_ref[...])
    i = jax.nn.sigmoid(_dot(xcb, wig_ref[...]) + big_ref[...])
    log_a = -LRU_C * r * jax.nn.softplus(-lam_ref[...])
    a = jnp.exp(log_a)
    bv = jnp.sqrt(1.0 - jnp.exp(2.0 * log_a)) * (i * xc)

    row = lax.broadcasted_iota(jnp.int32, (ts, LRU_WIDTH), 0)
    d = 1
    while d < ts:
        a_sh = jnp.where(row >= d, pltpu.roll(a, d, 0), 1.0)
        b_sh = jnp.where(row >= d, pltpu.roll(bv, d, 0), 0.0)
        bv = a * b_sh + bv
        a = a * a_sh
        d *= 2
    h = a * h_s[...] + bv
    h_s[...] = h[ts - 1:ts, :]
    out_ref[0] = (h * jax.nn.gelu(gate_ref[0])).astype(BF16)


def _block_diag(w):
    nb, bi, bo = w.shape
    out = jnp.zeros((nb * bi, nb * bo), w.dtype)
    for k in range(nb):
        out = out.at[k * bi:(k + 1) * bi, k * bo:(k + 1) * bo].set(w[k])
    return out


def _rg_lru(xb, gate, conv_w, conv_b, w_rg, b_rg, w_ig, b_ig, lam, ts):
    b, s, c = xb.shape
    row = lambda a: a[None, :]
    args = [conv_w, row(conv_b), _block_diag(w_rg).astype(BF16), row(b_rg),
            _block_diag(w_ig).astype(BF16), row(b_ig), row(lam)]
    tb = pl.BlockSpec((1, ts, c), lambda i, j: (i, j, 0))
    full = lambda a: pl.BlockSpec(a.shape, lambda i, j: (0,) * a.ndim)
    return pl.pallas_call(
        functools.partial(_lru_kernel, ts=ts),
        grid=(b, s // ts),
        in_specs=[tb, tb] + [full(a) for a in args],
        out_specs=tb,
        out_shape=jax.ShapeDtypeStruct((b, s, c), BF16),
        scratch_shapes=[pltpu.VMEM((ts + 8, c), F32), pltpu.VMEM((1, c), F32)],
        compiler_params=_cparams(("parallel", "arbitrary")),
        name="rg_lru",
    )(xb, gate, *args)


def _outproj_kernel(x_ref, att_ref, rec_ref, wo_ref, g_ref, wpq_ref, k1_ref, k2_ref,
                    x1_ref, h2t_ref, s1_ref, s2_ref):
    x1 = x_ref[...] + _dot(att_ref[...], wo_ref[0:512, :]) + _dot(rec_ref[...], wo_ref[512:1024, :])
    x1_ref[...] = x1
    h2f = _rms(x1, g_ref[...])
    h2t_ref[...] = h2f.T.astype(BF16)
    q = _dot(h2f.astype(BF16), wpq_ref[...]).astype(BF16)
    for h in range(PEER_HEADS):
        s1_ref[h] = _dot_nt(k1_ref[h], q[:, h * 256:h * 256 + PEER_HALF])
        s2_ref[h] = _dot_nt(k2_ref[h], q[:, h * 256 + PEER_HALF:(h + 1) * 256])


def _out_projection(x2, att, rec, w_out, g_ffn, w_pq, k1, k2, tm):
    n = x2.shape[0]
    g2 = g_ffn[None, :]
    tok = lambda w: pl.BlockSpec((tm, w), lambda i: (i, 0))
    full = lambda a: pl.BlockSpec(a.shape, lambda i: (0,) * a.ndim)
    sblk = pl.BlockSpec((PEER_HEADS, PEER_NKEYS, tm), lambda i: (0, 0, i))
    s_shape = jax.ShapeDtypeStruct((PEER_HEADS, PEER_NKEYS, n), F32)
    return pl.pallas_call(
        _outproj_kernel,
        grid=(n // tm,),
        in_specs=[tok(D_MODEL), tok(512), tok(512), full(w_out), full(g2), full(w_pq), full(k1), full(k2)],
        out_specs=[tok(D_MODEL), pl.BlockSpec((D_MODEL, tm), lambda i: (0, i)), sblk, sblk],
        out_shape=[jax.ShapeDtypeStruct((n, D_MODEL), F32), jax.ShapeDtypeStruct((D_MODEL, n), BF16),
                   s_shape, s_shape],
        compiler_params=_cparams(("parallel",)),
        name="out_projection",
    )(x2, att, rec, w_out, g2, w_pq, k1, k2)


def _top_values(s, k):
    vals = []
    for _ in range(k):
        m = jnp.max(s, axis=0, keepdims=True)
        vals.append(m)
        s = jnp.where(s == m, -jnp.inf, s)
    return vals


def _kth_largest(s, k):
    kth = None
    above = jnp.zeros((1, s.shape[1]), F32)
    for _ in range(k):
        m = jnp.max(s, axis=0, keepdims=True)
        hit = s == m
        kth = m if kth is None else jnp.where(above < k, m, kth)
        above = above + jnp.sum(jnp.where(hit, 1.0, 0.0), axis=0, keepdims=True)
        s = jnp.where(hit, -jnp.inf, s)
    return kth


def _peer_select_kernel(s1_ref, s2_ref, cnt_ref, p1_ref, rank_ref, p2_ref, v1_s, v2_s):
    s1 = s1_ref[0]
    s2 = s2_ref[0]
    t = s1.shape[1]
    v1 = _top_values(s1, PEER_TOPK)
    for i in range(PEER_TOPK):
        v1_s[i:i + 1, :] = v1[i]
    v1_all = v1_s[...]
    v2 = []
    cur = s2
    rank = jnp.full(s2.shape, float(PEER_TOPK), F32)
    for j in range(PEER_TOPK):
        m = jnp.max(cur, axis=0, keepdims=True)
        v2.append(m)
        hit = cur == m
        rank = jnp.where(hit, float(j), rank)
        cur = jnp.where(hit, -jnp.inf, cur)
        v2_s[j:j + 1, :] = m
    v2_all = v2_s[...]

    m1, m2 = v1[0], v2[0]
    e1_all = jnp.exp(v1_all - m1)
    e2_all = jnp.exp(v2_all - m2)
    row8 = lax.broadcasted_iota(jnp.int32, (SUBLANES, t), 0)
    slabs = [(v1[0] + v2_all[0:8], e1_all[0:1] * e2_all[0:8]),
             (v1[0] + v2_all[8:16], e1_all[0:1] * e2_all[8:16])]
    for i in range(1, 8):
        keep = row8 < PEER_TOPK // (i + 1)
        slabs.append((jnp.where(keep, v1[i] + v2_all[0:8], -jnp.inf), e1_all[i:i + 1] * e2_all[0:8]))
    slabs.append((v1_all[8:16] + v2[0], e1_all[8:16] * e2_all[0:1]))
    cand = jnp.concatenate([c for c, _ in slabs], axis=0)
    thr = _kth_largest(cand, PEER_TOPK)
    z = jnp.sum(jnp.where(cand >= thr, jnp.concatenate([e for _, e in slabs], axis=0), 0.0), axis=0, keepdims=True)

    cnt_top = jnp.zeros((PEER_TOPK, t), F32)
    for j in range(PEER_TOPK):
        cnt_top = cnt_top + jnp.where(v1_all + v2[j] >= thr, 1.0, 0.0)
    cnt = jnp.zeros(s1.shape, F32)
    for i in range(PEER_TOPK):
        cnt = jnp.where(s1 == v1[i], cnt_top[i:i + 1], cnt)
    cnt_ref[0] = cnt
    p1_ref[0] = jnp.exp(s1 - m1) / z
    p2 = jnp.exp(s2 - m2)
    for c in range(t // LANES):
        rank_ref[0, c] = rank[:, c * LANES:(c + 1) * LANES]
        p2_ref[0, c] = p2[:, c * LANES:(c + 1) * LANES]


def _peer_select(s1t, s2t, tb):
    n = s1t.shape[2]
    blk = pl.BlockSpec((1, PEER_NKEYS, tb), lambda h, i: (h, 0, i))
    f32s = jax.ShapeDtypeStruct(s1t.shape, F32)
    tiles = jax.ShapeDtypeStruct((PEER_HEADS, n // LANES, PEER_NKEYS, LANES), F32)
    tile_blk = pl.BlockSpec((1, tb // LANES, PEER_NKEYS, LANES), lambda h, i: (h, i, 0, 0))
    return pl.pallas_call(
        _peer_select_kernel,
        grid=(PEER_HEADS, n // tb),
        in_specs=[blk, blk],
        out_specs=[blk, blk, tile_blk, tile_blk],
        out_shape=[f32s, f32s, tiles, tiles],
        scratch_shapes=[pltpu.VMEM((PEER_TOPK, tb), F32), pltpu.VMEM((PEER_TOPK, tb), F32)],
        compiler_params=_cparams(("parallel", "parallel")),
        name="peer_select",
    )(s1t, s2t)


def _gelu_exact(x):
    return 0.5 * x * (1.0 + lax.erf(x * (2.0 ** -0.5)))


def _peer_dense_kernel(h2t_ref, u_ref, vt_ref, cnt_ref, p1_ref, rank_ref, p2_ref, out_ref,
                       acc_s, act_s, wg_s, cnt_s, p1_s, *, rows_per_step):
    e = pl.program_id(1)

    @pl.when(e == 0)
    def _():
        acc_s[...] = jnp.zeros_like(acc_s)

    act = _dot(u_ref[...], h2t_ref[...])
    n_tiles = act_s.shape[0]
    tb = n_tiles * LANES
    for tt in range(n_tiles):
        act_s[tt] = act[:, tt * LANES:(tt + 1) * LANES]
    for h in range(PEER_HEADS):
        for r in range(rows_per_step):
            i = h * rows_per_step + r
            row_c = cnt_ref[h, pl.ds(e * rows_per_step + r, 1), :]
            row_p = p1_ref[h, pl.ds(e * rows_per_step + r, 1), :]
            for tt in range(tb // LANES):
                ts = slice(tt * LANES, (tt + 1) * LANES)
                cnt_s[tt, i] = jnp.broadcast_to(row_c[:, ts], (SUBLANES, LANES))
                p1_s[tt, i] = jnp.broadcast_to(row_p[:, ts], (SUBLANES, LANES))

    def token_tile(tt, carry):
        key_tiles = PEER_NKEYS // SUBLANES
        for k0 in range(0, key_tiles, GATE_BLOCK):
            for r0 in range(0, rows_per_step, GATE_BLOCK):
                kk = range(k0, k0 + GATE_BLOCK)
                rr = range(r0, r0 + GATE_BLOCK)
                w = {(k, r): jnp.zeros((SUBLANES, LANES), F32) for k in kk for r in rr}
                for h in range(PEER_HEADS):
                    rank = {k: rank_ref[h, tt, k * SUBLANES:(k + 1) * SUBLANES, :] for k in kk}
                    p2 = {k: p2_ref[h, tt, k * SUBLANES:(k + 1) * SUBLANES, :] for k in kk}
                    for r in rr:
                        cnt = cnt_s[tt, h * rows_per_step + r]
                        p1 = p1_s[tt, h * rows_per_step + r]
                        for k in kk:
                            w[k, r] = w[k, r] + p1 * jnp.where(rank[k] < cnt, p2[k], 0.0)
                for r in rr:
                    for k in range(k0, k0 + GATE_BLOCK, 2):
                        es = slice(r * PEER_NKEYS + k * SUBLANES, r * PEER_NKEYS + (k + 2) * SUBLANES)
                        w2 = jnp.concatenate([w[k, r], w[k + 1, r]], axis=0)
                        wg_s[tt, es, :] = (w2 * _gelu_exact(act_s[tt, es, :])).astype(BF16)
        return carry

    lax.fori_loop(0, n_tiles, token_tile, 0)
    wg = jnp.concatenate([wg_s[tt] for tt in range(n_tiles)], axis=1)
    acc_s[...] += _dot(vt_ref[...], wg)

    @pl.when(e == pl.num_programs(1) - 1)
    def _():
        out_ref[...] = acc_s[...].T


def _peer_dense(h2t, u_bf, vt_bf, cnt, p1, rank, p2, tb, eb):
    n = h2t.shape[1]
    rows = eb // PEER_NKEYS
    rowblk = pl.BlockSpec((PEER_HEADS, PEER_NKEYS, tb), lambda i, e: (0, 0, i))
    tileblk = pl.BlockSpec((PEER_HEADS, tb // LANES, PEER_NKEYS, LANES), lambda i, e: (0, i, 0, 0))
    staged = pltpu.VMEM((tb // LANES, PEER_HEADS * rows, SUBLANES, LANES), F32)
    return pl.pallas_call(
        functools.partial(_peer_dense_kernel, rows_per_step=rows),
        grid=(n // tb, PEER_EXPERTS // eb),
        in_specs=[pl.BlockSpec((D_MODEL, tb), lambda i, e: (0, i)),
                  pl.BlockSpec((eb, D_MODEL), lambda i, e: (e, 0)),
                  pl.BlockSpec((D_MODEL, eb), lambda i, e: (0, e)),
                  rowblk, rowblk, tileblk, tileblk],
        out_specs=pl.BlockSpec((tb, D_MODEL), lambda i, e: (i, 0)),
        out_shape=jax.ShapeDtypeStruct((n, D_MODEL), F32),
        scratch_shapes=[pltpu.VMEM((D_MODEL, tb), F32), pltpu.VMEM((tb // LANES, eb, LANES), F32),
                        pltpu.VMEM((tb // LANES, eb, LANES), BF16), staged, staged],
        compiler_params=_cparams(("parallel", "arbitrary")),
        name="peer_dense",
    )(h2t, u_bf, vt_bf, cnt, p1, rank, p2)


def _final_kernel(x1_ref, peer_ref, p_ref, wg_ref, wp_ref, g_ref, out_ref, *, normalize):
    x2 = x1_ref[...] + peer_ref[...]
    gate = jax.nn.sigmoid(_dot(x2.astype(BF16), wg_ref[...]))
    x3 = x2 + gate * _dot(p_ref[...].astype(BF16), wp_ref[...])
    out_ref[...] = _rms(x3, g_ref[...]) if normalize else x3


def _final(x1, peer, p2, w_gate, w_ple, g_final, tm, normalize):
    n = x1.shape[0]
    g2 = g_final[None, :]
    tok = lambda w: pl.BlockSpec((tm, w), lambda i: (i, 0))
    full = lambda a: pl.BlockSpec(a.shape, lambda i: (0,) * a.ndim)
    return pl.pallas_call(
        functools.partial(_final_kernel, normalize=normalize),
        grid=(n // tm,),
        in_specs=[tok(D_MODEL), tok(D_MODEL), tok(PLE_DIM), full(w_gate), full(w_ple), full(g2)],
        out_specs=tok(D_MODEL),
        out_shape=jax.ShapeDtypeStruct((n, D_MODEL), F32),
        compiler_params=_cparams(("parallel",)),
        name="ple_final_norm",
    )(x1, peer, p2, w_gate, w_ple, g2)


def _layer(x2, p2, posf, b, s, normalize, g_mix, w_in, g_kv, w_uk, w_uv, conv_w, conv_b, w_rg, b_rg, w_ig, b_ig,
           lam, w_out, g_ffn, w_pq, k1, k2, u_tab, v_tab, w_ple, w_ple_gate, g_final):
    n = b * s
    tm = min(512, n)
    tq = 128
    kc = 256
    w_all, wuk_bd = _pack_in_weights(w_in, w_uk)
    qcat, kvcat, iq, ik2, iw, xb, gate = _in_projection(x2, posf, g_mix, w_all, wuk_bd, g_kv, tm)

    r3 = lambda a: a.reshape(b, s, a.shape[-1])
    iw_rows = iw[:, :IDX_HEADS].reshape(b, s // tq, tq, IDX_HEADS).swapaxes(2, 3).reshape(b, s // tq, 1, IDX_HEADS * tq)
    kvt = r3(kvcat)[:, :, :KV_LATENT].reshape(b, s // kc, kc, KV_LATENT).swapaxes(2, 3)
    wuvt = jnp.swapaxes(w_uv, 1, 2).astype(BF16)
    att = _dsa_attention(r3(iq), iw_rows, qcat.reshape(ATT_HEADS, b, s, 256), r3(ik2), r3(kvcat), kvt, wuvt, tq, kc, nb=4 if b % 4 == 0 else 1)
    rec = _rg_lru(r3(xb), r3(gate), conv_w, conv_b, w_rg, b_rg, w_ig, b_ig, lam, ts=min(256, s))

    x1, h2t, s1t, s2t = _out_projection(x2, att.reshape(n, 512), rec.reshape(n, 512), w_out.astype(BF16), g_ffn,
                                        w_pq.astype(BF16), k1.astype(BF16), k2.astype(BF16), tm)
    cnt, p1, rank, p2f = _peer_select(s1t, s2t, tb=min(512, n))
    peer = _peer_dense(h2t, u_tab.astype(BF16), v_tab.T.astype(BF16), cnt, p1, rank, p2f,
                       tb=min(512, n), eb=2048)
    return _final(x1, peer, p2, w_ple_gate.astype(BF16), w_ple.astype(BF16), g_final, min(1024, n), normalize)


def kernel(x, p, positions, g_mix, w_in, g_kv, w_uk, w_uv, conv_w, conv_b, w_rg, b_rg, w_ig, b_ig, lru_lambda,
           w_out, g_ffn, w_pq, peer_k1, peer_k2, peer_u, peer_v, w_ple, w_ple_gate, g_final):
    b, s, d = x.shape
    n = b * s
    depth = w_in.shape[0]
    posf = positions.astype(F32).reshape(n, 1)
    x2 = x.reshape(n, d)
    for i in range(depth):
        x2 = _layer(x2, p[i].reshape(n, PLE_DIM), posf, b, s, i == depth - 1, g_mix[i], w_in[i], g_kv[i], w_uk[i],
                    w_uv[i], conv_w[i], conv_b[i], w_rg[i], b_rg[i], w_ig[i], b_ig[i], lru_lambda[i], w_out[i],
                    g_ffn[i], w_pq[i], peer_k1[i], peer_k2[i], peer_u[i], peer_v[i], w_ple[i], w_ple_gate[i],
                    g_final)
    return x2.reshape(b, s, d)
```

```python
import functools

import jax
import jax.numpy as jnp
from jax import lax
from jax.experimental import pallas as pl
from jax.experimental.pallas import tpu as pltpu

D_MODEL = 1024
ATT_HEADS = 4
ATT_HEAD_DIM = 128
ROPE_DIM = 32
NOPE_DIM = ATT_HEAD_DIM - ROPE_DIM
KV_LATENT = 128
IDX_HEADS = 8
IDX_DIM = 64
IDX_ROPE_DIM = 16
IDX_TOPK_MAX = 256
LRU_WIDTH = 512
CONV_WIDTH = 4
LRU_C = 8.0
ROPE_THETA = 500000.0
PLE_DIM = 256
PEER_HEADS = 8
PEER_NKEYS = 128
PEER_EXPERTS = PEER_NKEYS * PEER_NKEYS
PEER_HALF = 128
PEER_TOPK = 16
EPS = 1e-6

LANES = 128
SUBLANES = 8
GATE_BLOCK = 4
INT_MIN = -2147483648
VMEM_LIMIT = 56 * 1024 * 1024

F32 = jnp.float32
BF16 = jnp.bfloat16

_C_QN = 0
_C_QR = _C_QN + 384
_C_QRR = _C_QR + 128
_C_CKV = _C_QRR + 128
_C_KR = _C_CKV + 128
_C_KRR = _C_KR + 128
_C_IQ = _C_KRR + 128
_C_IQR = _C_IQ + 512
_C_IK = _C_IQR + 512
_C_IKR = _C_IK + 128
_C_IW = _C_IKR + 128
_C_XB = _C_IW + 128
_C_GATE = _C_XB + 512
_C_TOTAL = _C_GATE + 512


def _cparams(sem):
    return pltpu.CompilerParams(dimension_semantics=sem, vmem_limit_bytes=VMEM_LIMIT)


def _rms(x, g):
    return x * lax.rsqrt(jnp.mean(x * x, axis=-1, keepdims=True) + EPS) * g


def _dot(a, b):
    return jnp.dot(a, b, preferred_element_type=F32)


def _dot_nt(a, b):
    return lax.dot_general(a, b, (((1,), (1,)), ((), ())), preferred_element_type=F32)


def _rot_half(w, half):
    return jnp.concatenate([-w[:, half:2 * half], w[:, :half]], axis=1)


def _pack_in_weights(w_in, w_uk):
    d = w_in.shape[0]
    o = 0
    wq = w_in[:, o:o + 512]; o += 512
    wckv = w_in[:, o:o + 128]; o += 128
    wkr = w_in[:, o:o + 32]; o += 32
    wiq = w_in[:, o:o + 512]; o += 512
    wik = w_in[:, o:o + 64]; o += 64
    wiw = w_in[:, o:o + 8]; o += 8
    wxb = w_in[:, o:o + 512]; o += 512
    wgate = w_in[:, o:o + 512]

    qn, qr, qrr = [], [], []
    for h in range(ATT_HEADS):
        wh = wq[:, h * 128:(h + 1) * 128]
        qr.append(wh[:, :ROPE_DIM])
        qrr.append(_rot_half(wh[:, :ROPE_DIM], ROPE_DIM // 2))
        qn.append(wh[:, ROPE_DIM:])
    iqr = []
    zpad = jnp.zeros((d, IDX_DIM - IDX_ROPE_DIM), w_in.dtype)
    for h in range(IDX_HEADS):
        wh = wiq[:, h * 64:(h + 1) * 64]
        iqr.append(jnp.concatenate([_rot_half(wh[:, :IDX_ROPE_DIM], IDX_ROPE_DIM // 2), zpad], axis=1))
    ikr = jnp.concatenate([_rot_half(wik[:, :IDX_ROPE_DIM], IDX_ROPE_DIM // 2), zpad], axis=1)
    cols = [
        jnp.concatenate(qn, axis=1),
        jnp.concatenate(qr, axis=1),
        jnp.concatenate(qrr, axis=1),
        wckv,
        jnp.tile(wkr, (1, 4)),
        jnp.tile(_rot_half(wkr, ROPE_DIM // 2), (1, 4)),
        wiq,
        jnp.concatenate(iqr, axis=1),
        jnp.tile(wik, (1, 2)),
        jnp.tile(ikr, (1, 2)),
        jnp.concatenate([wiw, jnp.zeros((d, 120), w_in.dtype)], axis=1),
        wxb,
        wgate,
    ]
    w_all = jnp.concatenate(cols, axis=1).astype(BF16)
    assert w_all.shape[1] == _C_TOTAL
    wuk_bd = jnp.zeros((ATT_HEADS * NOPE_DIM, ATT_HEADS * KV_LATENT), F32)
    for h in range(ATT_HEADS):
        wuk_bd = wuk_bd.at[h * NOPE_DIM:(h + 1) * NOPE_DIM, h * KV_LATENT:(h + 1) * KV_LATENT].set(w_uk[h].T)
    return w_all, wuk_bd.astype(BF16)


def _rope_freq_rows():
    inv_a = ROPE_THETA ** (-jnp.arange(0, ROPE_DIM, 2, dtype=F32) / ROPE_DIM)
    inv_i = ROPE_THETA ** (-jnp.arange(0, IDX_ROPE_DIM, 2, dtype=F32) / IDX_ROPE_DIM)
    row_a = jnp.tile(inv_a, LANES // inv_a.shape[0])[None, :]
    head_i = jnp.concatenate([inv_i, inv_i, jnp.zeros((IDX_DIM - IDX_ROPE_DIM,), F32)])
    row_i = jnp.tile(head_i, LANES // IDX_DIM)[None, :]
    return row_a, row_i


def _inproj_kernel(x_ref, pos_ref, g_ref, fa_ref, fi_ref, w_ref, wuk_ref, gkv_ref,
                   qcat_ref, kvcat_ref, iq_ref, ik_ref, iw_ref, xb_ref, gate_ref):
    hb = _rms(x_ref[...], g_ref[...]).astype(BF16)
    pos = pos_ref[...]

    y = _dot(hb, w_ref[...])

    def proj(c0, width):
        return y[:, c0:c0 + width]

    ang_a = pos * fa_ref[...]
    cos_a, sin_a = jnp.cos(ang_a), jnp.sin(ang_a)
    ang_i = pos * fi_ref[...]
    cos_i, sin_i = jnp.cos(ang_i), jnp.sin(ang_i)

    q_lat = _dot(proj(_C_QN, 384).astype(BF16), wuk_ref[...])
    q_rope = proj(_C_QR, 128) * cos_a + proj(_C_QRR, 128) * sin_a
    lane = lax.broadcasted_iota(jnp.int32, q_rope.shape, 1)
    for h in range(ATT_HEADS):
        qcat_ref[h, :, 0:128] = q_lat[:, h * 128:(h + 1) * 128].astype(BF16)
        keep = (lane >> 5) == h
        qcat_ref[h, :, 128:256] = jnp.where(keep, q_rope, 0.0).astype(BF16)

    kvcat_ref[:, 0:128] = _rms(proj(_C_CKV, 128), gkv_ref[...]).astype(BF16)
    kvcat_ref[:, 128:256] = (proj(_C_KR, 128) * cos_a + proj(_C_KRR, 128) * sin_a).astype(BF16)

    for c in range(4):
        sl = slice(c * 128, (c + 1) * 128)
        iq_ref[:, sl] = (proj(_C_IQ + c * 128, 128) * cos_i + proj(_C_IQR + c * 128, 128) * sin_i).astype(BF16)
    ik_ref[...] = (proj(_C_IK, 128) * cos_i + proj(_C_IKR, 128) * sin_i).astype(BF16)
    iw_ref[...] = proj(_C_IW, 128) * (IDX_HEADS ** -0.5 * IDX_DIM ** -0.5)
    xb_ref[...] = proj(_C_XB, 512)
    gate_ref[...] = proj(_C_GATE, 512)


def _in_projection(x2, posf, g_mix, w_all, wuk_bd, g_kv, tm):
    n = x2.shape[0]
    row_a, row_i = _rope_freq_rows()
    tok = lambda w: pl.BlockSpec((tm, w), lambda i: (i, 0))
    full = lambda a: pl.BlockSpec(a.shape, lambda i: (0,) * a.ndim)
    outs = [
        jax.ShapeDtypeStruct((ATT_HEADS, n, 256), BF16),
        jax.ShapeDtypeStruct((n, 256), BF16),
        jax.ShapeDtypeStruct((n, 512), BF16),
        jax.ShapeDtypeStruct((n, 128), BF16),
        jax.ShapeDtypeStruct((n, 128), F32),
        jax.ShapeDtypeStruct((n, 512), F32),
        jax.ShapeDtypeStruct((n, 512), F32),
    ]
    g2 = g_mix[None, :]
    gkv2 = g_kv[None, :]
    return pl.pallas_call(
        _inproj_kernel,
        grid=(n // tm,),
        in_specs=[tok(D_MODEL), tok(1), full(g2), full(row_a), full(row_i), full(w_all), full(wuk_bd), full(gkv2)],
        out_specs=[pl.BlockSpec((ATT_HEADS, tm, 256), lambda i: (0, i, 0))] + [tok(s.shape[1]) for s in outs[1:]],
        out_shape=outs,
        compiler_params=_cparams(("parallel",)),
        name="in_projection",
    )(x2, posf, g2, row_a, row_i, w_all, wuk_bd, gkv2)


def _ordered_bits_to_float(u):
    s = u ^ INT_MIN
    return pltpu.bitcast(s ^ ((s >> 31) & jnp.int32(0x7FFFFFFF)), F32)


def _tree(op, xs):
    while len(xs) > 1:
        xs = [op(xs[i], xs[i + 1]) if i + 1 < len(xs) else xs[i] for i in range(0, len(xs), 2)]
    return xs[0]


def _col_sum8(x):
    return jnp.sum(x.reshape(x.shape[0] // SUBLANES, SUBLANES, x.shape[1]), axis=0)


def _dsa_kernel(iq_ref, iw_ref, qcat_ref, ik_ref, kvcat_ref, kvt_ref, wuvt_ref, out_ref,
                iqm_s, score_s, lg_s, o_s, fv_s, fp_s, *, seq, tq, kc, topk, nb):
    j = pl.program_id(1)
    t0 = j * tq
    nk = (t0 + tq + kc - 1) // kc
    tpos = t0 + lax.broadcasted_iota(jnp.int32, (kc, tq), 1)
    row = lax.broadcasted_iota(jnp.int32, (kc, tq), 0)
    batch = range(nb)

    lane = lax.broadcasted_iota(jnp.int32, (tq, LANES), 1)
    for g in batch:
        for h in range(IDX_HEADS):
            pair = iq_ref[g, :, (h // 2) * 128:(h // 2 + 1) * 128]
            keep = (lane < IDX_DIM) if h % 2 == 0 else (lane >= IDX_DIM)
            iqm_s[g, h * tq:(h + 1) * tq, :] = jnp.where(keep, pair, jnp.zeros_like(pair))

    def score_chunk(c, carry):
        k0 = pl.multiple_of(c * kc, kc)
        for g in batch:
            r = jnp.maximum(_dot_nt(ik_ref[g, pl.ds(k0, kc), :], iqm_s[g]), 0.0) * iw_ref[g, 0]
            sc = r[:, 0:tq]
            for h in range(1, IDX_HEADS):
                sc = sc + r[:, h * tq:(h + 1) * tq]
            score_s[g, pl.ds(k0, kc), :] = jnp.where(k0 + row <= tpos, sc, -jnp.inf)
        return carry

    lax.fori_loop(0, nk, score_chunk, 0)

    def fold(fn, op, reduce_rows, init):
        def body(c, accs):
            k0 = pl.multiple_of(c * kc, kc)
            out = []
            for g in batch:
                v = fn(g, score_s[g, pl.ds(k0, kc), :], k0)
                out.append(op(accs[g], _tree(op, [v[i * SUBLANES:(i + 1) * SUBLANES] for i in range(kc // SUBLANES)])))
            return tuple(out)
        init_acc = jnp.full((SUBLANES, tq), init, jnp.asarray(init).dtype)
        accs = lax.fori_loop(0, nk, body, tuple(init_acc for _ in batch))
        return [reduce_rows(a, axis=0, keepdims=True) for a in accs]

    def count(pred):
        return fold(lambda g, blk, k0: pred(g, blk, k0).astype(jnp.int32), jnp.add, jnp.sum, jnp.int32(0))

    def bit_pass(count_ge, i, t_us):
        cand_us = [t_u | jnp.left_shift(jnp.int32(1), 31 - i) for t_u in t_us]
        cnts = count_ge([_ordered_bits_to_float(c) for c in cand_us])
        return tuple(jnp.where(cnts[g] >= topk, cand_us[g], t_us[g]) for g in batch)

    def search(n_chunks, _):
        def count_ge(cands):
            accs = [jnp.zeros((SUBLANES, tq), jnp.int32) for _ in batch]
            for c in range(n_chunks):
                for g in batch:
                    hit = (score_s[g, c * kc:(c + 1) * kc, :] >= cands[g]).astype(jnp.int32)
                    accs[g] = accs[g] + _tree(jnp.add, [hit[i * SUBLANES:(i + 1) * SUBLANES] for i in range(kc // SUBLANES)])
            return [jnp.sum(a, axis=0, keepdims=True) for a in accs]

        return lax.fori_loop(0, 32, functools.partial(bit_pass, count_ge),
                             tuple(jnp.zeros((1, tq), jnp.int32) for _ in batch))

    t_us = lax.switch(nk - 1, [functools.partial(search, n) for n in range(1, seq // kc + 1)], 0)
    thr = [_ordered_bits_to_float(jnp.where((t_u >> 23) == 0, jnp.int32(0x00800000), t_u)) for t_u in t_us]

    n_ge = count(lambda g, blk, k0: blk >= thr[g])
    for g in batch:
        fv_s[g] = thr[g]
        fp_s[g] = jnp.full((1, tq), seq, jnp.int32)
    extra = _tree(jnp.maximum, [jnp.max(n) for n in n_ge]) - topk

    @pl.when(extra > 0)
    def _():
        def drop_one(i, carry):
            fv, fp, kept = carry

            def kept_scores(g, blk, k0):
                pos = k0 + row
                return jnp.where(blk == fv[g], jnp.where(pos < fp[g], blk, jnp.inf), jnp.where(blk > fv[g], blk, jnp.inf))

            low = fold(kept_scores, jnp.minimum, jnp.min, jnp.float32(jnp.inf))

            def kept_pos_at_low(g, blk, k0):
                pos = k0 + row
                kept_pos = jnp.where(blk == fv[g], jnp.where(pos < fp[g], pos, -1), jnp.where(blk > fv[g], pos, -1))
                return jnp.where(blk == low[g], kept_pos, -1)

            last = fold(kept_pos_at_low, jnp.maximum, jnp.max, jnp.int32(-1))
            over = [kept[g] > topk for g in batch]
            return (tuple(jnp.where(over[g], low[g], fv[g]) for g in batch),
                    tuple(jnp.where(over[g], last[g], fp[g]) for g in batch),
                    tuple(jnp.where(over[g], kept[g] - 1, kept[g]) for g in batch))

        start = (tuple(thr), tuple(jnp.full((1, tq), seq, jnp.int32) for _ in batch), tuple(n_ge))
        fv, fp, _ = lax.fori_loop(0, extra, drop_one, start)
        for g in batch:
            fv_s[g] = fv[g]
            fp_s[g] = fp[g]

    floor_v = [fv_s[g] for g in batch]
    floor_p = [fp_s[g] for g in batch]
    qcat = [qcat_ref[:, g].reshape(ATT_HEADS * tq, 256) for g in batch]
    scale = ATT_HEAD_DIM ** -0.5

    def logit_chunk(c, m8s):
        k0 = pl.multiple_of(c * kc, kc)
        out = []
        for g in batch:
            blk = score_s[g, pl.ds(k0, kc), :]
            sel = jnp.where(blk == floor_v[g], (k0 + row < floor_p[g]).astype(jnp.int32),
                            (blk > floor_v[g]).astype(jnp.int32))
            bias = jnp.where(sel > 0, 0.0, -jnp.inf)
            lg = _dot_nt(kvcat_ref[g, pl.ds(k0, kc), :], qcat[g]) * scale
            lg = lg + jnp.concatenate([bias] * ATT_HEADS, axis=1)
            lg_s[g, pl.ds(k0, kc), :] = lg
            out.append(jnp.maximum(m8s[g], jnp.max(lg.reshape(kc // SUBLANES, SUBLANES, ATT_HEADS * tq), axis=0)))
        return tuple(out)

    m8s = lax.fori_loop(0, nk, logit_chunk,
                        tuple(jnp.full((SUBLANES, ATT_HEADS * tq), -jnp.inf, F32) for _ in batch))
    m = [jnp.max(m8, axis=0, keepdims=True) for m8 in m8s]
    o_s[...] = jnp.zeros_like(o_s)

    def pv_chunk(c, den8s):
        k0 = pl.multiple_of(c * kc, kc)
        out = []
        for g in batch:
            p = jnp.exp(lg_s[g, pl.ds(k0, kc), :] - m[g])
            o_s[g] += _dot(kvt_ref[g, c], p.astype(BF16))
            out.append(den8s[g] + _col_sum8(p))
        return tuple(out)

    den8s = lax.fori_loop(0, nk, pv_chunk, tuple(jnp.zeros((SUBLANES, ATT_HEADS * tq), F32) for _ in batch))
    for g in batch:
        o = (o_s[g] / jnp.sum(den8s[g], axis=0, keepdims=True)).astype(BF16)
        outs = [_dot(wuvt_ref[h], o[:, h * tq:(h + 1) * tq]) for h in range(ATT_HEADS)]
        out_ref[g] = jnp.concatenate(outs, axis=0).T.astype(BF16)


def _dsa_attention(iq, iw_rows, qcat, ik2, kvcat, kvt, wuvt, tq, kc, nb):
    b, s, _ = iq.shape
    topk = min(IDX_TOPK_MAX, s // 4)
    assert s & (s - 1) == 0, "sequence length must be a power of two"
    assert kc >= topk and s % kc == 0 and s % tq == 0 and b % nb == 0
    qb = lambda w: pl.BlockSpec((nb, tq, w), lambda i, j: (i, j, 0))
    kb = lambda w: pl.BlockSpec((nb, s, w), lambda i, j: (i, 0, 0))
    per_batch = lambda shape, dt: pltpu.VMEM((nb,) + shape, dt)
    return pl.pallas_call(
        functools.partial(_dsa_kernel, seq=s, tq=tq, kc=kc, topk=topk, nb=nb),
        grid=(b // nb, s // tq),
        in_specs=[qb(512),
                  pl.BlockSpec((nb, 1, 1, IDX_HEADS * tq), lambda i, j: (i, j, 0, 0)),
                  pl.BlockSpec((ATT_HEADS, nb, tq, 256), lambda i, j: (0, i, j, 0)),
                  kb(128), kb(256),
                  pl.BlockSpec((nb, s // kc, KV_LATENT, kc), lambda i, j: (i, 0, 0, 0)),
                  pl.BlockSpec(wuvt.shape, lambda i, j: (0, 0, 0))],
        out_specs=qb(512),
        out_shape=jax.ShapeDtypeStruct((b, s, 512), BF16),
        scratch_shapes=[per_batch((IDX_HEADS * tq, LANES), BF16),
                        per_batch((s, tq), F32),
                        per_batch((s, ATT_HEADS * tq), F32),
                        per_batch((KV_LATENT, ATT_HEADS * tq), F32),
                        per_batch((1, tq), F32), per_batch((1, tq), jnp.int32)],
        compiler_params=_cparams(("parallel", "parallel")),
        name="dsa_attention",
    )(iq, iw_rows, qcat, ik2, kvcat, kvt, wuvt)


def _lru_kernel(xb_ref, gate_ref, cw_ref, cb_ref, wrg_ref, brg_ref, wig_ref, big_ref, lam_ref, out_ref,
                xpad_s, h_s, *, ts):
    @pl.when(pl.program_id(1) == 0)
    def _():
        xpad_s[0:8, :] = jnp.zeros((8, LRU_WIDTH), F32)
        h_s[...] = jnp.zeros_like(h_s)

    xb = xb_ref[0]
    xpad_s[8:8 + ts, :] = xb
    xc = cb_ref[...]
    for w in range(CONV_WIDTH):
        off = 8 - (CONV_WIDTH - 1) + w
        xc = xc + cw_ref[w:w + 1, :] * xpad_s[off:off + ts, :]
    xpad_s[0:8, :] = xb[ts - 8:ts, :]

    xcb = xc.astype(BF16)
    r = jax.nn.sigmoid(_dot(xcb, wrg_ref[...]) + brg_ref[...])
    i = jax.nn.sigmoid(_dot(xcb, wig_ref[...]) + big_ref[...])
    log_a = -LRU_C * r * jax.nn.softplus(-lam_ref[...])
    a = jnp.exp(log_a)
    bv = jnp.sqrt(1.0 - jnp.exp(2.0 * log_a)) * (i * xc)

    row = lax.broadcasted_iota(jnp.int32, (ts, LRU_WIDTH), 0)
    d = 1
    while d < ts:
        a_sh = jnp.where(row >= d, pltpu.roll(a, d, 0), 1.0)
        b_sh = jnp.where(row >= d, pltpu.roll(bv, d, 0), 0.0)
        bv = a * b_sh + bv
        a = a * a_sh
        d *= 2
    h = a * h_s[...] + bv
    h_s[...] = h[ts - 1:ts, :]
    out_ref[0] = (h * jax.nn.gelu(gate_ref[0])).astype(BF16)


def _block_diag(w):
    nb, bi, bo = w.shape
    out = jnp.zeros((nb * bi, nb * bo), w.dtype)
    for k in range(nb):
        out = out.at[k * bi:(k + 1) * bi, k * bo:(k + 1) * bo].set(w[k])
    return out


def _rg_lru(xb, gate, conv_w, conv_b, w_rg, b_rg, w_ig, b_ig, lam, ts):
    b, s, c = xb.shape
    row = lambda a: a[None, :]
    args = [conv_w, row(conv_b), _block_diag(w_rg).astype(BF16), row(b_rg),
            _block_diag(w_ig).astype(BF16), row(b_ig), row(lam)]
    tb = pl.BlockSpec((1, ts, c), lambda i, j: (i, j, 0))
    full = lambda a: pl.BlockSpec(a.shape, lambda i, j: (0,) * a.ndim)
    return pl.pallas_call(
        functools.partial(_lru_kernel, ts=ts),
        grid=(b, s // ts),
        in_specs=[tb, tb] + [full(a) for a in args],
        out_specs=tb,
        out_shape=jax.ShapeDtypeStruct((b, s, c), BF16),
        scratch_shapes=[pltpu.VMEM((ts + 8, c), F32), pltpu.VMEM((1, c), F32)],
        compiler_params=_cparams(("parallel", "arbitrary")),
        name="rg_lru",
    )(xb, gate, *args)


def _outproj_kernel(x_ref, att_ref, rec_ref, wo_ref, g_ref, wpq_ref, k1_ref, k2_ref,
                    x1_ref, h2t_ref, s1_ref, s2_ref):
    x1 = x_ref[...] + _dot(att_ref[...], wo_ref[0:512, :]) + _dot(rec_ref[...], wo_ref[512:1024, :])
    x1_ref[...] = x1
    h2f = _rms(x1, g_ref[...])
    h2t_ref[...] = h2f.T.astype(BF16)
    q = _dot(h2f.astype(BF16), wpq_ref[...]).astype(BF16)
    for h in range(PEER_HEADS):
        s1_ref[h] = _dot_nt(k1_ref[h], q[:, h * 256:h * 256 + PEER_HALF])
        s2_ref[h] = _dot_nt(k2_ref[h], q[:, h * 256 + PEER_HALF:(h + 1) * 256])


def _out_projection(x2, att, rec, w_out, g_ffn, w_pq, k1, k2, tm):
    n = x2.shape[0]
    g2 = g_ffn[None, :]
    tok = lambda w: pl.BlockSpec((tm, w), lambda i: (i, 0))
    full = lambda a: pl.BlockSpec(a.shape, lambda i: (0,) * a.ndim)
    sblk = pl.BlockSpec((PEER_HEADS, PEER_NKEYS, tm), lambda i: (0, 0, i))
    s_shape = jax.ShapeDtypeStruct((PEER_HEADS, PEER_NKEYS, n), F32)
    return pl.pallas_call(
        _outproj_kernel,
        grid=(n // tm,),
        in_specs=[tok(D_MODEL), tok(512), tok(512), full(w_out), full(g2), full(w_pq), full(k1), full(k2)],
        out_specs=[tok(D_MODEL), pl.BlockSpec((D_MODEL, tm), lambda i: (0, i)), sblk, sblk],
        out_shape=[jax.ShapeDtypeStruct((n, D_MODEL), F32), jax.ShapeDtypeStruct((D_MODEL, n), BF16),
                   s_shape, s_shape],
        compiler_params=_cparams(("parallel",)),
        name="out_projection",
    )(x2, att, rec, w_out, g2, w_pq, k1, k2)


def _top_values(s, k):
    vals = []
    for _ in range(k):
        m = jnp.max(s, axis=0, keepdims=True)
        vals.append(m)
        s = jnp.where(s == m, -jnp.inf, s)
    return vals


def _kth_largest(s, k):
    kth = None
    above = jnp.zeros((1, s.shape[1]), F32)
    for _ in range(k):
        m = jnp.max(s, axis=0, keepdims=True)
        hit = s == m
        kth = m if kth is None else jnp.where(above < k, m, kth)
        above = above + jnp.sum(jnp.where(hit, 1.0, 0.0), axis=0, keepdims=True)
        s = jnp.where(hit, -jnp.inf, s)
    return kth


def _peer_select_kernel(s1_ref, s2_ref, cnt_ref, p1_ref, rank_ref, p2_ref, v1_s, v2_s):
    s1 = s1_ref[0]
    s2 = s2_ref[0]
    t = s1.shape[1]
    v1 = _top_values(s1, PEER_TOPK)
    for i in range(PEER_TOPK):
        v1_s[i:i + 1, :] = v1[i]
    v1_all = v1_s[...]
    v2 = []
    cur = s2
    rank = jnp.full(s2.shape, float(PEER_TOPK), F32)
    for j in range(PEER_TOPK):
        m = jnp.max(cur, axis=0, keepdims=True)
        v2.append(m)
        hit = cur == m
        rank = jnp.where(hit, float(j), rank)
        cur = jnp.where(hit, -jnp.inf, cur)
        v2_s[j:j + 1, :] = m
    v2_all = v2_s[...]

    m1, m2 = v1[0], v2[0]
    e1_all = jnp.exp(v1_all - m1)
    e2_all = jnp.exp(v2_all - m2)
    row8 = lax.broadcasted_iota(jnp.int32, (SUBLANES, t), 0)
    slabs = [(v1[0] + v2_all[0:8], e1_all[0:1] * e2_all[0:8]),
             (v1[0] + v2_all[8:16], e1_all[0:1] * e2_all[8:16])]
    for i in range(1, 8):
        keep = row8 < PEER_TOPK // (i + 1)
        slabs.append((jnp.where(keep, v1[i] + v2_all[0:8], -jnp.inf), e1_all[i:i + 1] * e2_all[0:8]))
    slabs.append((v1_all[8:16] + v2[0], e1_all[8:16] * e2_all[0:1]))
    cand = jnp.concatenate([c for c, _ in slabs], axis=0)
    thr = _kth_largest(cand, PEER_TOPK)
    z = jnp.sum(jnp.where(cand >= thr, jnp.concatenate([e for _, e in slabs], axis=0), 0.0), axis=0, keepdims=True)

    cnt_top = jnp.zeros((PEER_TOPK, t), F32)
    for j in range(PEER_TOPK):
        cnt_top = cnt_top + jnp.where(v1_all + v2[j] >= thr, 1.0, 0.0)
    cnt = jnp.zeros(s1.shape, F32)
    for i in range(PEER_TOPK):
        cnt = jnp.where(s1 == v1[i], cnt_top[i:i + 1], cnt)
    cnt_ref[0] = cnt
    p1_ref[0] = jnp.exp(s1 - m1) / z
    p2 = jnp.exp(s2 - m2)
    for c in range(t // LANES):
        rank_ref[0, c] = rank[:, c * LANES:(c + 1) * LANES]
        p2_ref[0, c] = p2[:, c * LANES:(c + 1) * LANES]


def _peer_select(s1t, s2t, tb):
    n = s1t.shape[2]
    blk = pl.BlockSpec((1, PEER_NKEYS, tb), lambda h, i: (h, 0, i))
    f32s = jax.ShapeDtypeStruct(s1t.shape, F32)
    tiles = jax.ShapeDtypeStruct((PEER_HEADS, n // LANES, PEER_NKEYS, LANES), F32)
    tile_blk = pl.BlockSpec((1, tb // LANES, PEER_NKEYS, LANES), lambda h, i: (h, i, 0, 0))
    return pl.pallas_call(
        _peer_select_kernel,
        grid=(PEER_HEADS, n // tb),
        in_specs=[blk, blk],
        out_specs=[blk, blk, tile_blk, tile_blk],
        out_shape=[f32s, f32s, tiles, tiles],
        scratch_shapes=[pltpu.VMEM((PEER_TOPK, tb), F32), pltpu.VMEM((PEER_TOPK, tb), F32)],
        compiler_params=_cparams(("parallel", "parallel")),
        name="peer_select",
    )(s1t, s2t)


def _gelu_exact(x):
    return 0.5 * x * (1.0 + lax.erf(x * (2.0 ** -0.5)))


def _peer_dense_kernel(h2t_ref, u_ref, v_ref, cnt_ref, p1_ref, rank_ref, p2_ref, out_ref,
                       acc_s, act_s, wg_s, cnt_s, p1_s, *, rows_per_step):
    e = pl.program_id(1)

    @pl.when(e == 0)
    def _():
        acc_s[...] = jnp.zeros_like(acc_s)

    act = _dot(u_ref[...], h2t_ref[...])
    n_tiles = act_s.shape[0]
    tb = n_tiles * LANES
    for tt in range(n_tiles):
        act_s[tt] = act[:, tt * LANES:(tt + 1) * LANES]
    for h in range(PEER_HEADS):
        for r in range(rows_per_step):
            i = h * rows_per_step + r
            row_c = cnt_ref[h, pl.ds(e * rows_per_step + r, 1), :]
            row_p = p1_ref[h, pl.ds(e * rows_per_step + r, 1), :]
            for tt in range(tb // LANES):
                ts = slice(tt * LANES, (tt + 1) * LANES)
                cnt_s[tt, i] = jnp.broadcast_to(row_c[:, ts], (SUBLANES, LANES))
                p1_s[tt, i] = jnp.broadcast_to(row_p[:, ts], (SUBLANES, LANES))

    def token_tile(tt, carry):
        key_tiles = PEER_NKEYS // SUBLANES
        for k0 in range(0, key_tiles, GATE_BLOCK):
            for r0 in range(0, rows_per_step, GATE_BLOCK):
                kk = range(k0, k0 + GATE_BLOCK)
                rr = range(r0, r0 + GATE_BLOCK)
                w = {(k, r): jnp.zeros((SUBLANES, LANES), F32) for k in kk for r in rr}
                for h in range(PEER_HEADS):
                    rank = {k: rank_ref[h, tt, k * SUBLANES:(k + 1) * SUBLANES, :] for k in kk}
                    p2 = {k: p2_ref[h, tt, k * SUBLANES:(k + 1) * SUBLANES, :] for k in kk}
                    for r in rr:
                        cnt = cnt_s[tt, h * rows_per_step + r]
                        p1 = p1_s[tt, h * rows_per_step + r]
                        for k in kk:
                            w[k, r] = w[k, r] + p1 * jnp.where(rank[k] < cnt, p2[k], 0.0)
                for r in rr:
                    for k in range(k0, k0 + GATE_BLOCK, 2):
                        es = slice(r * PEER_NKEYS + k * SUBLANES, r * PEER_NKEYS + (k + 2) * SUBLANES)
                        w2 = jnp.concatenate([w[k, r], w[k + 1, r]], axis=0)
                        wg_s[tt, es, :] = (w2 * _gelu_exact(act_s[tt, es, :])).astype(BF16)
        return carry

    lax.fori_loop(0, n_tiles, token_tile, 0)
    wg = jnp.concatenate([wg_s[tt] for tt in range(n_tiles)], axis=1)
    acc_s[...] += lax.dot_general(v_ref[...], wg, (((0,), (0,)), ((), ())), preferred_element_type=F32)

    @pl.when(e == pl.num_programs(1) - 1)
    def _():
        out_ref[...] = acc_s[...].T


def _peer_dense(h2t, u_bf, v_bf, cnt, p1, rank, p2, tb, eb):
    n = h2t.shape[1]
    rows = eb // PEER_NKEYS
    rowblk = pl.BlockSpec((PEER_HEADS, PEER_NKEYS, tb), lambda i, e: (0, 0, i))
    tileblk = pl.BlockSpec((PEER_HEADS, tb // LANES, PEER_NKEYS, LANES), lambda i, e: (0, i, 0, 0))
    staged = pltpu.VMEM((tb // LANES, PEER_HEADS * rows, SUBLANES, LANES), F32)
    return pl.pallas_call(
        functools.partial(_peer_dense_kernel, rows_per_step=rows),
        grid=(n // tb, PEER_EXPERTS // eb),
        in_specs=[pl.BlockSpec((D_MODEL, tb), lambda i, e: (0, i)),
                  pl.BlockSpec((eb, D_MODEL), lambda i, e: (e, 0)),
                  pl.BlockSpec((eb, D_MODEL), lambda i, e: (e, 0)),
                  rowblk, rowblk, tileblk, tileblk],
        out_specs=pl.BlockSpec((tb, D_MODEL), lambda i, e: (i, 0)),
        out_shape=jax.ShapeDtypeStruct((n, D_MODEL), F32),
        scratch_shapes=[pltpu.VMEM((D_MODEL, tb), F32), pltpu.VMEM((tb // LANES, eb, LANES), F32),
                        pltpu.VMEM((tb // LANES, eb, LANES), BF16), staged, staged],
        compiler_params=_cparams(("parallel", "arbitrary")),
        name="peer_dense",
    )(h2t, u_bf, v_bf, cnt, p1, rank, p2)


def _final_kernel(x1_ref, peer_ref, p_ref, wg_ref, wp_ref, g_ref, out_ref, *, normalize):
    x2 = x1_ref[...] + peer_ref[...]
    gate = jax.nn.sigmoid(_dot(x2.astype(BF16), wg_ref[...]))
    x3 = x2 + gate * _dot(p_ref[...].astype(BF16), wp_ref[...])
    out_ref[...] = _rms(x3, g_ref[...]) if normalize else x3


def _final(x1, peer, p2, w_gate, w_ple, g_final, tm, normalize):
    n = x1.shape[0]
    g2 = g_final[None, :]
    tok = lambda w: pl.BlockSpec((tm, w), lambda i: (i, 0))
    full = lambda a: pl.BlockSpec(a.shape, lambda i: (0,) * a.ndim)
    return pl.pallas_call(
        functools.partial(_final_kernel, normalize=normalize),
        grid=(n // tm,),
        in_specs=[tok(D_MODEL), tok(D_MODEL), tok(PLE_DIM), full(w_gate), full(w_ple), full(g2)],
        out_specs=tok(D_MODEL),
        out_shape=jax.ShapeDtypeStruct((n, D_MODEL), F32),
        compiler_params=_cparams(("parallel",)),
        name="ple_final_norm",
    )(x1, peer, p2, w_gate, w_ple, g2)


def _layer(x2, p2, posf, b, s, normalize, g_mix, w_in, g_kv, w_uk, w_uv, conv_w, conv_b, w_rg, b_rg, w_ig, b_ig,
           lam, w_out, g_ffn, w_pq, k1, k2, u_tab, v_tab, w_ple, w_ple_gate, g_final):
    n = b * s
    tm = min(512, n)
    tq = 128
    kc = 256
    w_all, wuk_bd = _pack_in_weights(w_in, w_uk)
    qcat, kvcat, iq, ik2, iw, xb, gate = _in_projection(x2, posf, g_mix, w_all, wuk_bd, g_kv, tm)

    r3 = lambda a: a.reshape(b, s, a.shape[-1])
    iw_rows = iw[:, :IDX_HEADS].reshape(b, s // tq, tq, IDX_HEADS).swapaxes(2, 3).reshape(b, s // tq, 1, IDX_HEADS * tq)
    kvt = r3(kvcat)[:, :, :KV_LATENT].reshape(b, s // kc, kc, KV_LATENT).swapaxes(2, 3)
    wuvt = jnp.swapaxes(w_uv, 1, 2).astype(BF16)
    att = _dsa_attention(r3(iq), iw_rows, qcat.reshape(ATT_HEADS, b, s, 256), r3(ik2), r3(kvcat), kvt, wuvt, tq, kc, nb=4 if b % 4 == 0 else 1)
    rec = _rg_lru(r3(xb), r3(gate), conv_w, conv_b, w_rg, b_rg, w_ig, b_ig, lam, ts=min(256, s))

    x1, h2t, s1t, s2t = _out_projection(x2, att.reshape(n, 512), rec.reshape(n, 512), w_out.astype(BF16), g_ffn,
                                        w_pq.astype(BF16), k1.astype(BF16), k2.astype(BF16), tm)
    cnt, p1, rank, p2f = _peer_select(s1t, s2t, tb=min(512, n))
    peer = _peer_dense(h2t, u_tab.astype(BF16), v_tab.astype(BF16), cnt, p1, rank, p2f,
                       tb=min(512, n), eb=2048)
    return _final(x1, peer, p2, w_ple_gate.astype(BF16), w_ple.astype(BF16), g_final, min(1024, n), normalize)


def kernel(x, p, positions, g_mix, w_in, g_kv, w_uk, w_uv, conv_w, conv_b, w_rg, b_rg, w_ig, b_ig, lru_lambda,
           w_out, g_ffn, w_pq, peer_k1, peer_k2, peer_u, peer_v, w_ple, w_ple_gate, g_final):
    b, s, d = x.shape
    n = b * s
    depth = w_in.shape[0]
    posf = positions.astype(F32).reshape(n, 1)
    x2 = x.reshape(n, d)
    for i in range(depth):
        x2 = _layer(x2, p[i].reshape(n, PLE_DIM), posf, b, s, i == depth - 1, g_mix[i], w_in[i], g_kv[i], w_uk[i],
                    w_uv[i], conv_w[i], conv_b[i], w_rg[i], b_rg[i], w_ig[i], b_ig[i], lru_lambda[i], w_out[i],
                    g_ffn[i], w_pq[i], peer_k1[i], peer_k2[i], peer_u[i], peer_v[i], w_ple[i], w_ple_gate[i],
                    g_final)
    return x2.reshape(b, s, d)
```
